```python
import math
import jax, jax.numpy as jnp
from jax import lax
import numpy as np

D_MODEL = 1024
BATCH = 4
SEQ = 4096
DEPTH = 1

NSA_HEADS = 8
NSA_KV_GROUPS = 2
NSA_HPG = NSA_HEADS // NSA_KV_GROUPS
NSA_HEAD_DIM = 64
CMP_BLOCK = 32
CMP_STRIDE = 16
CMP_HIDDEN = 256
SLC_BLOCK = 64
SLC_TOPN = 16
WINDOW = 512
Q_BLOCK = 64
FORCE_BONUS = 1e4
GLA_HEADS = 4
GLA_DK = 64
GLA_DV = 128
GLA_GATE_RANK = 16
GLA_TAU = 16.0
GLA_CHUNK = 64
N_EXPERTS = 256
TOP_K = 8
N_EXPERT_GROUPS = 8
TOPK_GROUPS = 4
EXPERT_HIDDEN = 256
SHARED_HIDDEN = 256
ROUTE_SCALE = 2.5
MOE_BLOCK = 128
MIX_WIDTH = NSA_HEADS * NSA_HEAD_DIM + GLA_HEADS * GLA_DV
LN_EPS = 1e-5
NEG = -1e30
SPLIT_SIZES = (NSA_HEADS * NSA_HEAD_DIM,
               NSA_KV_GROUPS * NSA_HEAD_DIM,
               NSA_KV_GROUPS * NSA_HEAD_DIM,
               NSA_KV_GROUPS * NSA_HEAD_DIM,
               NSA_KV_GROUPS * NSA_HEAD_DIM,
               NSA_KV_GROUPS * NSA_HEAD_DIM,
               NSA_KV_GROUPS * NSA_HEAD_DIM,
               NSA_HEADS * 3,
               GLA_HEADS * GLA_DK,
               GLA_HEADS * GLA_DK,
               GLA_HEADS * GLA_DV,
               GLA_GATE_RANK,
               GLA_HEADS * GLA_DV)
IN_COLS = sum(SPLIT_SIZES)

kernel_name = 'hybrid_nsa_gla_moe_deepnorm'


def _normalize(x):
    xf = x.astype(jnp.float32)
    mu = jnp.mean(xf, -1, keepdims=True)
    var = jnp.mean(jnp.square(xf - mu), -1, keepdims=True)
    return ((xf - mu) * lax.rsqrt(var + LN_EPS)).astype(x.dtype)


def _layer_norm(x, g, b):
    return _normalize(x) * g + b


def _alibi_slopes(n):
    return jnp.asarray(np.power(2.0, -8.0 * np.arange(1, n + 1) / n).astype(np.float32))


def _masked_softmax(s, mask):
    s = jnp.where(mask, s, NEG)
    m = jnp.max(s, -1, keepdims=True)
    e = jnp.where(mask, jnp.exp(s - m), 0.0)
    return e / jnp.maximum(jnp.sum(e, -1, keepdims=True), 1e-30)


def _compress(k, pos, w1, w2):
    B, T, G, dh = k.shape
    r = CMP_BLOCK // CMP_STRIDE
    n_sub = T // CMP_STRIDE
    sub = k.reshape(B, n_sub, CMP_STRIDE, G, dh)
    blocks = jnp.concatenate([sub[:, i:n_sub - r + 1 + i] for i in range(r)], axis=2)
    blocks = blocks + pos[None, None, :, None, :]
    n_cmp = blocks.shape[1]
    flat = blocks.transpose(0, 1, 3, 2, 4).reshape(B, n_cmp, G, CMP_BLOCK * dh)
    return jax.nn.silu(flat @ w1) @ w2


def _nsa(q, k_cmp, v_cmp, k_slc, v_slc, k_win, v_win, gate_logits,
         cmp_pos_k, cmp_w1_k, cmp_w2_k, cmp_pos_v, cmp_w1_v, cmp_w2_v):
    B, T = q.shape[:2]
    G, P, dh = NSA_KV_GROUPS, NSA_HPG, NSA_HEAD_DIM
    dt = q.dtype
    q = (q * dh ** -0.5).reshape(B, T, G, P, dh)
    gates = jax.nn.sigmoid(gate_logits.astype(jnp.float32)).astype(dt).reshape(B, T, G, P, 3)
    slopes = _alibi_slopes(NSA_HEADS).reshape(G, P)

    kc = _compress(k_cmp, cmp_pos_k, cmp_w1_k, cmp_w2_k)
    vc = _compress(v_cmp, cmp_pos_v, cmp_w1_v, cmp_w2_v)
    n_cmp = kc.shape[1]
    n_slc = T // SLC_BLOCK
    n_sel = min(SLC_TOPN, n_slc)
    cs = jnp.arange(n_cmp) * CMP_STRIDE
    ce = cs + CMP_BLOCK - 1
    ss = jnp.arange(n_slc) * SLC_BLOCK
    se = ss + SLC_BLOCK - 1
    overlap = jnp.clip(jnp.minimum(ce[:, None], se[None]) - jnp.maximum(cs[:, None], ss[None]) + 1, 0)
    overlap = overlap.astype(jnp.float32) / CMP_BLOCK
    cmp_end = ce.astype(jnp.float32)

    ks_blk = k_slc.reshape(B, n_slc, SLC_BLOCK, G, dh).transpose(0, 3, 1, 2, 4)
    vs_blk = v_slc.reshape(B, n_slc, SLC_BLOCK, G, dh).transpose(0, 3, 1, 2, 4)
    kw_pad = jnp.pad(k_win, ((0, 0), (WINDOW, 0), (0, 0), (0, 0)))
    vw_pad = jnp.pad(v_win, ((0, 0), (WINDOW, 0), (0, 0), (0, 0)))
    b_ix = jnp.arange(B)[:, None, None, None]
    g_ix = jnp.arange(G)[None, :, None, None]
    j_blk = jnp.arange(n_slc)

    def block(qb_idx):
        q0 = qb_idx * Q_BLOCK
        qb = lax.dynamic_slice_in_dim(q, q0, Q_BLOCK, axis=1)
        gb = lax.dynamic_slice_in_dim(gates, q0, Q_BLOCK, axis=1)
        t = q0 + jnp.arange(Q_BLOCK)
        tf = t.astype(jnp.float32)

        dist_c = tf[:, None] - cmp_end[None]
        s_c = jnp.einsum('bqgpd,bngd->bgpqn', qb, kc).astype(jnp.float32) - slopes[:, :, None, None] * dist_c
        p_cmp = _masked_softmax(s_c, dist_c >= 0)
        o_cmp = jnp.einsum('bgpqn,bngd->bqgpd', p_cmp.astype(dt), vc)

        imp = jnp.einsum('bgpqn,nj->bgqj', p_cmp, overlap)
        t_blk = (t // SLC_BLOCK)[:, None]
        causal_blk = j_blk[None] <= t_blk
        forced = (j_blk[None] == 0) | (j_blk[None] == t_blk) | (j_blk[None] == t_blk - 1)
        score = jnp.where(causal_blk, imp + jnp.where(forced, FORCE_BONUS, 0.0), NEG)
        _, idx = lax.top_k(score, n_sel)
        kg = ks_blk[b_ix, g_ix, idx]
        vg = vs_blk[b_ix, g_ix, idx]
        key_pos = idx[..., None] * SLC_BLOCK + jnp.arange(SLC_BLOCK)
        dist_s = (tf[None, None, :, None, None] - key_pos.astype(jnp.float32))[:, :, None]
        s_s = jnp.einsum('bqgpd,bgqnsd->bgpqns', qb, kg).astype(jnp.float32) - slopes[:, :, None, None, None] * dist_s
        shp = s_s.shape
        flat_shape = shp[:4] + (shp[4] * shp[5],)
        mask_s = jnp.broadcast_to(dist_s >= 0, shp).reshape(flat_shape)
        p_s = _masked_softmax(s_s.reshape(flat_shape), mask_s).reshape(shp)
        o_slc = jnp.einsum('bgpqns,bgqnsd->bqgpd', p_s.astype(dt), vg)

        kwb = lax.dynamic_slice_in_dim(kw_pad, q0, WINDOW + Q_BLOCK, axis=1)
        vwb = lax.dynamic_slice_in_dim(vw_pad, q0, WINDOW + Q_BLOCK, axis=1)
        wpos = q0 - WINDOW + jnp.arange(WINDOW + Q_BLOCK)
        dist_w = tf[:, None] - wpos[None].astype(jnp.float32)
        mask_w = (dist_w >= 0) & (dist_w < WINDOW) & (wpos[None] >= 0)
        s_w = jnp.einsum('bqgpd,bkgd->bgpqk', qb, kwb).astype(jnp.float32) - slopes[:, :, None, None] * dist_w
        p_w = _masked_softmax(s_w, mask_w)
        o_win = jnp.einsum('bgpqk,bkgd->bqgpd', p_w.astype(dt), vwb)

        return gb[..., 0:1] * o_cmp + gb[..., 1:2] * o_slc + gb[..., 2:3] * o_win

    out = lax.map(block, jnp.arange(T // Q_BLOCK))
    return out.transpose(1, 0, 2, 3, 4, 5).reshape(B, T, NSA_HEADS * dh)


def _gla_chunk_step(S, inp):
    q, k, v, g = inp
    b = jnp.cumsum(g, axis=2)
    C = q.shape[2]
    causal = jnp.tril(jnp.ones((C, C), bool))
    diff = b[:, :, :, None, :] - b[:, :, None, :, :]
    decay = jnp.where(causal[:, :, None], jnp.exp(jnp.minimum(diff, 0.0)), 0.0)
    attn = jnp.einsum('bhtd,bhsd,bhtsd->bhts', q, k, decay)
    o = jnp.einsum('bhtd,bhde->bhte', q * jnp.exp(b), S) + jnp.einsum('bhts,bhse->bhte', attn, v)
    b_last = b[:, :, -1:, :]
    S = jnp.exp(b_last[:, :, 0, :])[..., None] * S + jnp.einsum('bhcd,bhce->bhde', k * jnp.exp(b_last - b), v)
    return S, o


def _gla(q, k, v, gate_lr, r, w_a2, b_a2, norm_w):
    B, T = q.shape[:2]
    H, dk, dv, C = GLA_HEADS, GLA_DK, GLA_DV, GLA_CHUNK
    dt = q.dtype
    log_a = jax.nn.log_sigmoid((gate_lr @ w_a2 + b_a2).astype(jnp.float32)) / GLA_TAU

    def chunks(a, d):
        return a.astype(jnp.float32).reshape(B, T // C, C, H, d).transpose(1, 0, 3, 2, 4)

    xs = (chunks(q * dk ** -0.5, dk), chunks(k, dk), chunks(v, dv), chunks(log_a, dk))
    S0 = jnp.zeros((B, H, dk, dv), jnp.float32)
    _, o = lax.scan(_gla_chunk_step, S0, xs)
    o = o.transpose(1, 0, 3, 2, 4).reshape(B, T, H, dv)
    o = _normalize(o) * norm_w.astype(jnp.float32)
    return (o.reshape(B, T, H * dv) * jax.nn.silu(r.astype(jnp.float32))).astype(dt)


def _token_mixer(h, w_in, cmp_pos_k, cmp_w1_k, cmp_w2_k, cmp_pos_v, cmp_w1_v, cmp_w2_v,
                 gla_w_a2, gla_b_a2, gla_norm_w, w_o):
    B, T, _ = h.shape
    proj = h @ w_in
    points, acc = [], 0
    for sz in SPLIT_SIZES[:-1]:
        acc += sz
        points.append(acc)
    (nq, kc_, vc_, ks_, vs_, kw_, vw_, ng, gq, gk, gv, glr, gr) = jnp.split(proj, points, axis=-1)
    kv = lambda a: a.reshape(B, T, NSA_KV_GROUPS, NSA_HEAD_DIM)
    o_nsa = _nsa(nq.reshape(B, T, NSA_HEADS, NSA_HEAD_DIM), kv(kc_), kv(vc_), kv(ks_), kv(vs_),
                 kv(kw_), kv(vw_), ng.reshape(B, T, NSA_HEADS, 3),
                 cmp_pos_k, cmp_w1_k, cmp_w2_k, cmp_pos_v, cmp_w1_v, cmp_w2_v)
    o_gla = _gla(gq, gk, gv, glr, gr, gla_w_a2, gla_b_a2, gla_norm_w)
    return jnp.concatenate([o_nsa, o_gla], axis=-1) @ w_o


def _moe(h, w_router, router_bias, w_e_gate, w_e_up, w_e_down, w_s_gate, w_s_up, w_s_down):
    B, T, D = h.shape
    N = B * T
    dt = h.dtype
    xt = h.reshape(N, D)
    s = jax.nn.sigmoid((xt @ w_router).astype(jnp.float32))
    sb = s + router_bias.astype(jnp.float32)
    grp = sb.reshape(N, N_EXPERT_GROUPS, N_EXPERTS // N_EXPERT_GROUPS)
    grp_score = jnp.sum(lax.top_k(grp, 2)[0], -1)
    _, gidx = lax.top_k(grp_score, TOPK_GROUPS)
    gmask = jnp.sum(jax.nn.one_hot(gidx, N_EXPERT_GROUPS), axis=1) > 0
    emask = jnp.repeat(gmask, N_EXPERTS // N_EXPERT_GROUPS, axis=1)
    _, eidx = lax.top_k(jnp.where(emask, sb, NEG), TOP_K)
    w = jnp.take_along_axis(s, eidx, axis=1)
    w = w / jnp.sum(w, -1, keepdims=True) * ROUTE_SCALE

    NK = N * TOP_K
    flat_e = eidx.reshape(NK)
    flat_tok = jnp.repeat(jnp.arange(N, dtype=jnp.int32), TOP_K)
    flat_w = w.reshape(NK).astype(dt)
    order = jnp.argsort(flat_e)
    se, stok, sw = flat_e[order], flat_tok[order], flat_w[order]
    counts = jnp.bincount(flat_e, length=N_EXPERTS)
    padded = (counts + MOE_BLOCK - 1) // MOE_BLOCK * MOE_BLOCK
    pend = jnp.cumsum(padded)
    pstart = pend - padded
    start = jnp.cumsum(counts) - counts
    dest = pstart[se] + (jnp.arange(NK) - start[se])
    n_blocks = -(-NK // MOE_BLOCK) + N_EXPERTS
    R = n_blocks * MOE_BLOCK
    row_tok = jnp.full((R,), N, jnp.int32).at[dest].set(stok)
    row_w = jnp.zeros((R,), dt).at[dest].set(sw)
    blk_e = jnp.minimum(jnp.searchsorted(pend, jnp.arange(n_blocks) * MOE_BLOCK, side='right'), N_EXPERTS - 1)
    x_pad = jnp.concatenate([xt, jnp.zeros((1, D), dt)], axis=0)

    def step(acc, inp):
        tok, wt, e = inp
        xb = x_pad[tok]
        hid = jax.nn.silu(xb @ w_e_gate[e]) * (xb @ w_e_up[e])
        return acc.at[tok].add((hid @ w_e_down[e]) * wt[:, None]), None

    acc, _ = lax.scan(step, jnp.zeros((N + 1, D), dt),
                      (row_tok.reshape(n_blocks, MOE_BLOCK), row_w.reshape(n_blocks, MOE_BLOCK), blk_e))
    shared = (jax.nn.silu(xt @ w_s_gate) * (xt @ w_s_up)) @ w_s_down
    return (acc[:N] + shared).reshape(B, T, D)


def setup_inputs(seed: int = 0) -> dict:
    key = jax.random.key(seed)
    ks = jax.random.split(key, 32)
    f32 = jnp.float32
    L, D = DEPTH, D_MODEL
    beta = (8.0 * DEPTH) ** -0.25
    nrm = lambda k, shape, scale: jax.random.normal(k, shape, f32) * scale
    return {
        'x': nrm(ks[0], (BATCH, SEQ, D), 1.0),
        'c': nrm(ks[1], (BATCH, D), 1.0),
        'w_mod': nrm(ks[2], (L, D, 6 * D), 0.1 * D ** -0.5),
        'b_mod': nrm(ks[3], (L, 6 * D), 0.02),
        'w_in': nrm(ks[4], (L, D, IN_COLS), D ** -0.5),
        'cmp_pos_k': nrm(ks[5], (L, CMP_BLOCK, NSA_HEAD_DIM), 0.02),
        'cmp_w1_k': nrm(ks[6], (L, CMP_BLOCK * NSA_HEAD_DIM, CMP_HIDDEN), (CMP_BLOCK * NSA_HEAD_DIM) ** -0.5),
        'cmp_w2_k': nrm(ks[7], (L, CMP_HIDDEN, NSA_HEAD_DIM), CMP_HIDDEN ** -0.5),
        'cmp_pos_v': nrm(ks[8], (L, CMP_BLOCK, NSA_HEAD_DIM), 0.02),
        'cmp_w1_v': nrm(ks[9], (L, CMP_BLOCK * NSA_HEAD_DIM, CMP_HIDDEN), (CMP_BLOCK * NSA_HEAD_DIM) ** -0.5),
        'cmp_w2_v': nrm(ks[10], (L, CMP_HIDDEN, NSA_HEAD_DIM), CMP_HIDDEN ** -0.5),
        'gla_w_a2': nrm(ks[11], (L, GLA_GATE_RANK, GLA_HEADS * GLA_DK), GLA_GATE_RANK ** -0.5),
        'gla_b_a2': nrm(ks[12], (L, GLA_HEADS * GLA_DK), 0.1),
        'gla_norm_w': 1.0 + nrm(ks[13], (L, GLA_DV), 0.02),
        'w_o': nrm(ks[14], (L, MIX_WIDTH, D), beta * MIX_WIDTH ** -0.5),
        'ln1_g': 1.0 + nrm(ks[15], (L, D), 0.02),
        'ln1_b': nrm(ks[16], (L, D), 0.02),
        'w_router': nrm(ks[17], (L, D, N_EXPERTS), D ** -0.5),
        'router_bias': nrm(ks[18], (L, N_EXPERTS), 0.01),
        'w_e_gate': nrm(ks[19], (L, N_EXPERTS, D, EXPERT_HIDDEN), D ** -0.5),
        'w_e_up': nrm(ks[20], (L, N_EXPERTS, D, EXPERT_HIDDEN), D ** -0.5),
        'w_e_down': nrm(ks[21], (L, N_EXPERTS, EXPERT_HIDDEN, D), beta * EXPERT_HIDDEN ** -0.5),
        'w_s_gate': nrm(ks[22], (L, D, SHARED_HIDDEN), D ** -0.5),
        'w_s_up': nrm(ks[23], (L, D, SHARED_HIDDEN), D ** -0.5),
        'w_s_down': nrm(ks[24], (L, SHARED_HIDDEN, D), beta * SHARED_HIDDEN ** -0.5),
        'ln2_g': 1.0 + nrm(ks[25], (L, D), 0.02),
        'ln2_b': nrm(ks[26], (L, D), 0.02),
    }


def reference(x, c, w_mod, b_mod, w_in, cmp_pos_k, cmp_w1_k, cmp_w2_k, cmp_pos_v, cmp_w1_v, cmp_w2_v,
              gla_w_a2, gla_b_a2, gla_norm_w, w_o, ln1_g, ln1_b, w_router, router_bias,
              w_e_gate, w_e_up, w_e_down, w_s_gate, w_s_up, w_s_down, ln2_g, ln2_b):
    alpha = (2.0 * DEPTH) ** 0.25
    for l in range(DEPTH):
        mod = jax.nn.silu(c) @ w_mod[l] + b_mod[l]
        sh1, sc1, ga1, sh2, sc2, ga2 = jnp.split(mod[:, None, :], 6, axis=-1)
        h = _normalize(x) * (1.0 + sc1) + sh1
        y = _token_mixer(h, w_in[l], cmp_pos_k[l], cmp_w1_k[l], cmp_w2_k[l], cmp_pos_v[l], cmp_w1_v[l],
                         cmp_w2_v[l], gla_w_a2[l], gla_b_a2[l], gla_norm_w[l], w_o[l])
        x = _layer_norm(alpha * x + (1.0 + ga1) * y, ln1_g[l], ln1_b[l])
        h = _normalize(x) * (1.0 + sc2) + sh2
        y = _moe(h, w_router[l], router_bias[l], w_e_gate[l], w_e_up[l], w_e_down[l],
                 w_s_gate[l], w_s_up[l], w_s_down[l])
        x = _layer_norm(alpha * x + (1.0 + ga2) * y, ln2_g[l], ln2_b[l])
    return x
```

```python
import functools

import jax
import jax.numpy as jnp
import numpy as np
from jax import lax
from jax.experimental import pallas as pl
from jax.experimental.pallas import tpu as pltpu

NSA_HEADS = 8
NSA_KV_GROUPS = 2
NSA_HPG = NSA_HEADS // NSA_KV_GROUPS
NSA_HEAD_DIM = 64
CMP_BLOCK = 32
CMP_STRIDE = 16
CMP_HIDDEN = 256
SLC_BLOCK = 64
SLC_TOPN = 16
WINDOW = 512
FORCE_BONUS = 1e4
GLA_HEADS = 4
GLA_DK = 64
GLA_DV = 128
GLA_GATE_RANK = 16
GLA_TAU = 16.0
N_EXPERTS = 256
TOP_K = 8
N_EXPERT_GROUPS = 8
TOPK_GROUPS = 4
EXPERT_HIDDEN = 256
ROUTE_SCALE = 2.5
LN_EPS = 1e-5
NEG = -1e30
TAKEN = -3e38

LANES = 128
VMEM_LIMIT = 56 * 1024 * 1024

TM_PROJ = 256
TQ_NSA = 128
TK_SLC = 512
GLA_CHUNK = 128
GLA_SUB = 16
GLA_EXP_CLAMP = 80.0
TM_OPROJ = 256
MOE_BLOCK = 128
TM_DISP = 128
TM_COMB = 128

BF16 = jnp.bfloat16
F32 = jnp.float32


def _cparams(n_axes):
    return pltpu.CompilerParams(
        dimension_semantics=("arbitrary",) * n_axes, vmem_limit_bytes=VMEM_LIMIT)


def _dot(a, b):
    return jnp.dot(a, b, preferred_element_type=F32)


def _dot_nt(a, b):
    return lax.dot_general(a, b, (((1,), (1,)), ((), ())), preferred_element_type=F32)


def _split(a):
    hi = a.astype(BF16)
    lo = (a - hi.astype(F32)).astype(BF16)
    return hi, lo


def _dot3(a, b):
    ah, al = _split(a)
    bh, bl = _split(b)
    return _dot(ah, bh) + _dot(ah, bl) + _dot(al, bh)


def _silu(x):
    return x * (1.0 / (1.0 + jnp.exp(-x)))


def _sigmoid(x):
    return 1.0 / (1.0 + jnp.exp(-x))


def _normalize_rows(x):
    mu = jnp.mean(x, axis=-1, keepdims=True)
    xc = x - mu
    var = jnp.mean(xc * xc, axis=-1, keepdims=True)
    return xc * lax.rsqrt(var + LN_EPS)


def _mod_kernel(c_ref, w_ref, b_ref, o_ref):
    o_ref[...] = _dot3(_silu(c_ref[...]), w_ref[...]) + b_ref[...]


def _mod_call(c_pad, w_mod, b_mod):
    rows, d = c_pad.shape
    n_out = w_mod.shape[1]
    return pl.pallas_call(
        _mod_kernel,
        grid=(n_out // d,),
        in_specs=[pl.BlockSpec((rows, d), lambda j: (0, 0)),
                  pl.BlockSpec((d, d), lambda j: (0, j)),
                  pl.BlockSpec((1, d), lambda j: (0, j))],
        out_specs=pl.BlockSpec((rows, d), lambda j: (0, j)),
        out_shape=jax.ShapeDtypeStruct((rows, n_out), F32),
        compiler_params=_cparams(1),
        name="mod",
    )(c_pad, w_mod, b_mod)


_PROJ_COLS = (("qn", 1024, BF16), ("kvn", 512, BF16), ("cmp", 256, F32), ("gate", 256, F32),
              ("gqk", 1024, F32), ("gv", 512, BF16), ("glr", 128, F32), ("gr", 512, F32))


def _inproj_kernel(x_ref, sh_ref, sc_ref, w_ref, wvt_ref,
                   qn_ref, kvn_ref, cmp_ref, gate_ref, gqk_ref, gv_ref, glr_ref, gr_ref, gvt_ref):
    h = _normalize_rows(x_ref[0]) * (1.0 + sc_ref[0]) + sh_ref[0]
    hb = h.astype(BF16)
    outs = (qn_ref, kvn_ref, cmp_ref, gate_ref, gqk_ref, gv_ref, glr_ref, gr_ref)
    c0 = 0
    for (name, width, dt), o_ref in zip(_PROJ_COLS, outs):
        p = _dot(hb, w_ref[:, c0:c0 + width])
        if name == "gate":
            p = _sigmoid(p)
        o_ref[0] = p.astype(dt)
        c0 += width
    gvt_ref[0] = _dot_nt(wvt_ref[...], hb).astype(BF16)


def _inproj_call(x, mod3, w_all, w_vt):
    B, T, D = x.shape
    tm = min(TM_PROJ, T)
    ctot = w_all.shape[1]
    out_shape = [jax.ShapeDtypeStruct((B, T, wdt), dt) for _, wdt, dt in _PROJ_COLS]
    out_shape.append(jax.ShapeDtypeStruct((B, w_vt.shape[0], T), BF16))
    out_specs = [pl.BlockSpec((1, tm, wdt), lambda b, i: (b, i, 0)) for _, wdt, _ in _PROJ_COLS]
    out_specs.append(pl.BlockSpec((1, w_vt.shape[0], tm), lambda b, i: (b, 0, i)))
    return pl.pallas_call(
        _inproj_kernel,
        grid=(B, T // tm),
        in_specs=[pl.BlockSpec((1, tm, D), lambda b, i: (b, i, 0)),
                  pl.BlockSpec((1, 1, D), lambda b, i: (b * 6 + 0, 0, 0)),
                  pl.BlockSpec((1, 1, D), lambda b, i: (b * 6 + 1, 0, 0)),
                  pl.BlockSpec((D, ctot), lambda b, i: (0, 0)),
                  pl.BlockSpec(w_vt.shape, lambda b, i: (0, 0))],
        out_specs=out_specs,
        out_shape=out_shape,
        compiler_params=_cparams(2),
        name="inproj",
    )(x, mod3, mod3, w_all, w_vt)


def _compress_kernel(sub_ref, pos_ref, w1_ref, w2_ref, o_ref):
    sub = sub_ref[0, 0, 0]
    n_sub = sub.shape[0]
    half = sub.shape[1]
    a_lo = (sub + pos_ref[0, 0:1, :]).astype(BF16)
    a_hi = (sub + pos_ref[0, 1:2, :]).astype(BF16)
    u = _dot(a_lo, w1_ref[0, 0:half, :].astype(BF16))
    v = _dot(a_hi, w1_ref[0, half:2 * half, :].astype(BF16))
    hid = u + pltpu.roll(v, n_sub - 1, 0)
    o_ref[0, 0, 0] = _dot(_silu(hid).astype(BF16), w2_ref[0].astype(BF16))


def _compress_call(sub, pos2, w1, w2):
    _, B, G, n_sub, half = sub.shape
    return pl.pallas_call(
        _compress_kernel,
        grid=(2, B, G),
        in_specs=[pl.BlockSpec((1, 1, 1, n_sub, half), lambda s, b, g: (s, b, g, 0, 0)),
                  pl.BlockSpec((1, 2, half), lambda s, b, g: (s, 0, 0)),
                  pl.BlockSpec((1,) + w1.shape[1:], lambda s, b, g: (s, 0, 0)),
                  pl.BlockSpec((1,) + w2.shape[1:], lambda s, b, g: (s, 0, 0))],
        out_specs=pl.BlockSpec((1, 1, 1, n_sub, w2.shape[2]), lambda s, b, g: (s, b, g, 0, 0)),
        out_shape=jax.ShapeDtypeStruct((2, B, G, n_sub, w2.shape[2]), F32),
        compiler_params=_cparams(3),
        name="compress",
    )(sub, pos2, w1, w2)


def _masked_softmax(s, mask):
    s = jnp.where(mask, s, NEG)
    m = jnp.max(s, axis=-1, keepdims=True)
    e = jnp.where(mask, jnp.exp(s - m), 0.0)
    return e / jnp.maximum(jnp.sum(e, axis=-1, keepdims=True), 1e-30)


def _nsa_kernel(q_ref, gate_ref, kc_ref, vc_ref, kv_ref, ov_ref, o_ref, st_ref, *, seq_len):
    g = pl.program_id(1)
    qi = pl.program_id(2)
    tq = q_ref.shape[1]
    P = NSA_HPG
    R = P * tq
    n_cp = kc_ref.shape[1]
    n_sp = ov_ref.shape[1]
    n_slc = seq_len // SLC_BLOCK
    n_sel = min(SLC_TOPN, n_slc)
    q0 = qi * tq

    qblk = q_ref[0]
    Q = jnp.concatenate([qblk[:, p * LANES:(p + 1) * LANES] for p in range(P)], axis=0)
    gscale = jnp.where(g == 0, 1.0, 2.0 ** (-P)).astype(F32)
    slope = jnp.concatenate(
        [jnp.full((tq, 1), 2.0 ** (-(p + 1)), F32) for p in range(P)], axis=0) * gscale
    t_q = q0 + lax.broadcasted_iota(jnp.int32, (tq, 1), 0)
    t_row = jnp.concatenate([t_q] * P, axis=0)
    t_rowf = t_row.astype(F32)

    n_idx = lax.broadcasted_iota(jnp.int32, (1, n_cp), 1)
    ce = (n_idx * CMP_STRIDE + (CMP_BLOCK - 1)).astype(F32)
    dist_c = t_rowf - ce
    s_c = _dot_nt(Q, kc_ref[0]) - slope * dist_c
    p_c = _masked_softmax(s_c, dist_c >= 0)
    o_cmp = _dot(p_c.astype(BF16), vc_ref[0])

    p_sum = p_c[0:tq]
    for p in range(1, P):
        p_sum = p_sum + p_c[p * tq:(p + 1) * tq]
    ph, plo = _split(p_sum)
    imp = _dot(ph, ov_ref[...]) + _dot(plo, ov_ref[...])
    j_idx = lax.broadcasted_iota(jnp.int32, (1, n_sp), 1)
    t_blk = t_q // SLC_BLOCK
    causal_blk = j_idx <= t_blk
    forced = (j_idx == 0) | (j_idx == t_blk) | (j_idx == t_blk - 1)
    score = jnp.where(causal_blk, imp + jnp.where(forced, FORCE_BONUS, 0.0), NEG)
    st_ref[...] = score.T
    s_all = st_ref[...]
    jrow = lax.broadcasted_iota(jnp.int32, (n_sp, tq), 0)

    def rank_body(jp, cnt):
        row = st_ref[pl.ds(jp, 1), :]
        ge = jnp.where(row >= s_all, 1.0, 0.0)
        gt = jnp.where(row > s_all, 1.0, 0.0)
        return cnt + jnp.where(jrow > jp, ge, gt)

    cnt = lax.fori_loop(0, n_slc, rank_body, jnp.zeros((n_sp, tq), F32))
    sel = jnp.where(cnt < n_sel, 1.0, 0.0).T.astype(BF16)

    tk = min(TK_SLC, seq_len)
    jrow_e = lax.broadcasted_iota(jnp.int32, (n_sp, tk), 0)
    kcol_e = lax.broadcasted_iota(jnp.int32, (n_sp, tk), 1)
    kcol = lax.broadcasted_iota(jnp.int32, (1, tk), 1)

    def slc_body(it, carry):
        m_i, l_i, acc = carry
        k0 = pl.multiple_of(it * tk, tk)
        kt = kv_ref[0, pl.ds(k0, tk), 0:LANES]
        vt = kv_ref[0, pl.ds(k0, tk), LANES:2 * LANES]
        expand = jnp.where((k0 + kcol_e) // SLC_BLOCK == jrow_e, 1.0, 0.0).astype(BF16)
        selk = _dot(sel, expand)
        selk = jnp.concatenate([selk] * P, axis=0)
        dist = t_rowf - (k0 + kcol).astype(F32)
        mask = (selk > 0.5) & (dist >= 0)
        s = jnp.where(mask, _dot_nt(Q, kt) - slope * dist, NEG)
        m_new = jnp.maximum(m_i, jnp.max(s, axis=-1, keepdims=True))
        e = jnp.where(mask, jnp.exp(s - m_new), 0.0)
        alpha = jnp.exp(m_i - m_new)
        l_new = alpha * l_i + jnp.sum(e, axis=-1, keepdims=True)
        acc_new = alpha * acc + _dot(e.astype(BF16), vt)
        return m_new, l_new, acc_new

    n_tiles = (q0 + tq + tk - 1) // tk
    init = (jnp.full((R, 1), NEG, F32), jnp.zeros((R, 1), F32), jnp.zeros((R, LANES), F32))
    _, l_s, acc_s = lax.fori_loop(0, n_tiles, slc_body, init)
    o_slc = acc_s / jnp.maximum(l_s, 1e-30)

    wlen = min(WINDOW + tq, seq_len)
    w0 = pl.multiple_of(jnp.minimum(jnp.maximum(q0 - WINDOW, 0), seq_len - wlen), tq)
    kw = kv_ref[0, pl.ds(w0, wlen), 2 * LANES:3 * LANES]
    vw = kv_ref[0, pl.ds(w0, wlen), 3 * LANES:4 * LANES]
    wpos = (w0 + lax.broadcasted_iota(jnp.int32, (1, wlen), 1)).astype(F32)
    dist_w = t_rowf - wpos
    mask_w = (dist_w >= 0) & (dist_w < WINDOW)
    p_w = _masked_softmax(_dot_nt(Q, kw) - slope * dist_w, mask_w)
    o_win = _dot(p_w.astype(BF16), vw)

    gates = gate_ref[0]
    lane = lax.broadcasted_iota(jnp.int32, (1, LANES), 1)
    live = (lane // NSA_HEAD_DIM) == g
    pieces = []
    for p in range(P):
        rows = slice(p * tq, (p + 1) * tq)
        o_p = (gates[:, 3 * p + 0:3 * p + 1] * o_cmp[rows]
               + gates[:, 3 * p + 1:3 * p + 2] * o_slc[rows]
               + gates[:, 3 * p + 2:3 * p + 3] * o_win[rows])
        pieces.append(jnp.where(live, o_p, 0.0))
    o_ref[0] = jnp.concatenate(pieces, axis=1).astype(o_ref.dtype)


def _nsa_call(qn, gates, kc, vc, kvn, overlap):
    B, T, _ = qn.shape
    G, P = NSA_KV_GROUPS, NSA_HPG
    tq = min(TQ_NSA, T)
    n_cp = kc.shape[1]
    n_sp = overlap.shape[1]
    return pl.pallas_call(
        functools.partial(_nsa_kernel, seq_len=T),
        grid=(B, G, T // tq),
        in_specs=[pl.BlockSpec((1, tq, P * LANES), lambda b, g, i: (b, i, g)),
                  pl.BlockSpec((1, tq, LANES), lambda b, g, i: (b, i, g)),
                  pl.BlockSpec((1, n_cp, LANES), lambda b, g, i: (b, 0, 0)),
                  pl.BlockSpec((1, n_cp, LANES), lambda b, g, i: (b, 0, 0)),
                  pl.BlockSpec((1, T, 4 * LANES), lambda b, g, i: (b, 0, 0)),
                  pl.BlockSpec((n_cp, n_sp), lambda b, g, i: (0, 0))],
        out_specs=pl.BlockSpec((1, tq, P * LANES), lambda b, g, i: (b, i, g)),
        out_shape=jax.ShapeDtypeStruct((B, T, G * P * LANES), BF16),
        scratch_shapes=[pltpu.VMEM((n_sp, tq), F32)],
        compiler_params=_cparams(3),
        name="nsa",
    )(qn, gates, kc, vc, kvn, overlap)


def _gla_kernel(qk_ref, v_ref, vt_ref, glr_ref, r_ref, wa_ref, ba_ref, nw_ref, o_ref, st_ref):
    ci = pl.program_id(1)
    C = qk_ref.shape[1]
    H = GLA_HEADS
    n_sub = C // GLA_SUB

    @pl.when(ci == 0)
    def _():
        st_ref[...] = jnp.zeros_like(st_ref)

    z = _dot3(glr_ref[0], wa_ref[...]) + ba_ref[...]
    gdec = -(jnp.maximum(-z, 0.0) + jnp.log(1.0 + jnp.exp(-jnp.abs(z)))) * (1.0 / GLA_TAU)
    rowi = lax.broadcasted_iota(jnp.int32, (C, 1), 0)
    b = gdec
    sh = 1
    while sh < C:
        b = b + jnp.where(rowi >= sh, pltpu.roll(b, sh, 0), 0.0)
        sh *= 2

    qk = qk_ref[0]
    r_i = lax.broadcasted_iota(jnp.int32, (C, C), 0)
    c_i = lax.broadcasted_iota(jnp.int32, (C, C), 1)
    causal = c_i <= r_i
    for h in range(H):
        cols = slice(h * LANES, (h + 1) * LANES)
        qh = qk[:, h * LANES:(h + 1) * LANES]
        kh = qk[:, (H + h) * LANES:(H + h + 1) * LANES]
        bh = b[:, cols]
        bl = bh[C - 1:C, :]
        st = st_ref[h]
        o_inter = _dot_nt((qh * jnp.exp(bh)).astype(BF16), st.astype(BF16))
        a_rows = []
        for i in range(n_sub):
            rs = slice(i * GLA_SUB, (i + 1) * GLA_SUB)
            ref_b = bh[i * GLA_SUB:i * GLA_SUB + 1, :]
            q_i = (qh[rs] * jnp.exp(bh[rs] - ref_b)).astype(BF16)
            k_i = (kh * jnp.exp(jnp.minimum(ref_b - bh, GLA_EXP_CLAMP))).astype(BF16)
            a_rows.append(_dot_nt(q_i, k_i))
        attn = jnp.where(causal, jnp.concatenate(a_rows, axis=0), 0.0)
        o_h = o_inter + _dot(attn.astype(BF16), v_ref[0, :, cols])
        kd = (kh * jnp.exp(bl - bh)).astype(BF16)
        st_ref[h] = st * jnp.exp(bl) + _dot(vt_ref[0, h * GLA_DV:(h + 1) * GLA_DV, :], kd)
        o_n = _normalize_rows(o_h) * nw_ref[...]
        o_ref[0, :, cols] = (o_n * _silu(r_ref[0, :, cols])).astype(o_ref.dtype)


def _gla_call(gqk, gv, gvt, glr, gr, wa_pad, ba_pad, norm_w):
    B, T, _ = gqk.shape
    H = GLA_HEADS
    C = min(GLA_CHUNK, T)
    return pl.pallas_call(
        _gla_kernel,
        grid=(B, T // C),
        in_specs=[pl.BlockSpec((1, C, 2 * H * LANES), lambda b, i: (b, i, 0)),
                  pl.BlockSpec((1, C, H * GLA_DV), lambda b, i: (b, i, 0)),
                  pl.BlockSpec((1, H * GLA_DV, C), lambda b, i: (b, 0, i)),
                  pl.BlockSpec((1, C, LANES), lambda b, i: (b, i, 0)),
                  pl.BlockSpec((1, C, H * GLA_DV), lambda b, i: (b, i, 0)),
                  pl.BlockSpec(wa_pad.shape, lambda b, i: (0, 0)),
                  pl.BlockSpec(ba_pad.shape, lambda b, i: (0, 0)),
                  pl.BlockSpec(norm_w.shape, lambda b, i: (0, 0))],
        out_specs=pl.BlockSpec((1, C, H * GLA_DV), lambda b, i: (b, i, 0)),
        out_shape=jax.ShapeDtypeStruct((B, T, H * GLA_DV), BF16),
        scratch_shapes=[pltpu.VMEM((H, GLA_DV, LANES), F32)],
        compiler_params=_cparams(2),
        name="gla",
    )(gqk, gv, gvt, glr, gr, wa_pad, ba_pad, norm_w)


def _oproj_kernel(on_ref, og_ref, x_ref, ga_ref, sh_ref, sc_ref, g1_ref, b1_ref, wo_ref,
                  wrh_ref, wrl_ref, rb_ref, tri_ref,
                  x1_ref, h2_ref, h2p_ref, eidx_ref, wgt_ref, pos_ref, cnt_ref, run_ref, *, alpha):
    first = (pl.program_id(0) == 0) & (pl.program_id(1) == 0)

    @pl.when(first)
    def _():
        run_ref[...] = jnp.zeros_like(run_ref)

    kn = on_ref.shape[2]
    y = _dot(on_ref[0], wo_ref[0:kn, :]) + _dot(og_ref[0], wo_ref[kn:, :])
    x1 = _normalize_rows(alpha * x_ref[0] + (1.0 + ga_ref[0]) * y) * g1_ref[...] + b1_ref[...]
    x1_ref[0] = x1
    h2 = _normalize_rows(x1) * (1.0 + sc_ref[0]) + sh_ref[0]
    hh, hl = _split(h2)
    h2_ref[0] = hh
    half = h2.shape[1] // 2
    bits = pltpu.bitcast(hh.astype(F32), jnp.uint32)
    h2p_ref[0] = (bits[:, 0:half] >> 16) | (bits[:, half:] & jnp.uint32(0xFFFF0000))

    logit = _dot_nt(wrh_ref[...], hh) + _dot_nt(wrh_ref[...], hl) + _dot_nt(wrl_ref[...], hh)
    s = _sigmoid(logit)
    sb = s + rb_ref[...]
    E, tm = s.shape
    gsz = E // N_EXPERT_GROUPS
    gi = lax.broadcasted_iota(jnp.int32, (gsz, tm), 0).astype(F32)
    gscore = []
    for gidx in range(N_EXPERT_GROUPS):
        blk = sb[gidx * gsz:(gidx + 1) * gsz]
        m1 = jnp.max(blk, axis=0, keepdims=True)
        i1 = jnp.min(jnp.where(blk == m1, gi, float(gsz)), axis=0, keepdims=True)
        m2 = jnp.max(jnp.where(gi == i1, TAKEN, blk), axis=0, keepdims=True)
        gscore.append(m1 + m2)
    masked = []
    for gidx in range(N_EXPERT_GROUPS):
        rank = jnp.zeros((1, tm), F32)
        for other in range(N_EXPERT_GROUPS):
            if other == gidx:
                continue
            beats = (gscore[other] >= gscore[gidx]) if other < gidx else (gscore[other] > gscore[gidx])
            rank = rank + jnp.where(beats, 1.0, 0.0)
        masked.append(jnp.where(rank < TOPK_GROUPS, sb[gidx * gsz:(gidx + 1) * gsz], NEG))
    ms = jnp.concatenate(masked, axis=0)
    ei = lax.broadcasted_iota(jnp.int32, (E, tm), 0).astype(F32)
    picks, idxs, vals = [], [], []
    for _ in range(TOP_K):
        m = jnp.max(ms, axis=0, keepdims=True)
        ix = jnp.min(jnp.where(ms == m, ei, float(E)), axis=0, keepdims=True)
        pick = ei == ix
        ms = jnp.where(pick, TAKEN, ms)
        picks.append(pick)
        idxs.append(ix)
        vals.append(jnp.sum(jnp.where(pick, s, 0.0), axis=0, keepdims=True))
    wsum = vals[0]
    for v in vals[1:]:
        wsum = wsum + v
    sel = jnp.zeros((E, tm), F32)
    for pick in picks:
        sel = sel + jnp.where(pick, 1.0, 0.0)
    pos_all = run_ref[:, 0:1] + _dot(sel.astype(BF16), tri_ref[...])
    run_ref[...] = run_ref[...] + jnp.sum(sel, axis=1, keepdims=True)
    cnt_ref[...] = run_ref[...]
    for k in range(TOP_K):
        eidx_ref[k:k + 1, :] = idxs[k].astype(jnp.int32)
        wgt_ref[k:k + 1, :] = vals[k] / wsum * ROUTE_SCALE
        pos_ref[k:k + 1, :] = jnp.sum(jnp.where(picks[k], pos_all, 0.0), axis=0,
                                      keepdims=True).astype(jnp.int32)


def _oproj_call(o_nsa, o_gla, x, mod3, ln_g, ln_b, wo_pad, wr_hi, wr_lo, rbias, tri, alpha):
    B, T, D = x.shape
    tm = min(TM_OPROJ, T)
    N = B * T
    E = wr_hi.shape[0]
    nt = T // tm
    tok = lambda b, i: (0, b * nt + i)
    const2 = lambda b, i: (0, 0)
    return pl.pallas_call(
        functools.partial(_oproj_kernel, alpha=alpha),
        grid=(B, nt),
        in_specs=[pl.BlockSpec((1, tm, o_nsa.shape[2]), lambda b, i: (b, i, 0)),
                  pl.BlockSpec((1, tm, o_gla.shape[2]), lambda b, i: (b, i, 0)),
                  pl.BlockSpec((1, tm, D), lambda b, i: (b, i, 0)),
                  pl.BlockSpec((1, 1, D), lambda b, i: (b * 6 + 2, 0, 0)),
                  pl.BlockSpec((1, 1, D), lambda b, i: (b * 6 + 3, 0, 0)),
                  pl.BlockSpec((1, 1, D), lambda b, i: (b * 6 + 4, 0, 0)),
                  pl.BlockSpec((1, D), const2),
                  pl.BlockSpec((1, D), const2),
                  pl.BlockSpec(wo_pad.shape, const2),
                  pl.BlockSpec(wr_hi.shape, const2),
                  pl.BlockSpec(wr_lo.shape, const2),
                  pl.BlockSpec(rbias.shape, const2),
                  pl.BlockSpec(tri.shape, const2)],
        out_specs=[pl.BlockSpec((1, tm, D), lambda b, i: (b, i, 0)),
                   pl.BlockSpec((1, tm, D), lambda b, i: (b, i, 0)),
                   pl.BlockSpec((1, tm, D // 2), lambda b, i: (b, i, 0)),
                   pl.BlockSpec((TOP_K, tm), tok),
                   pl.BlockSpec((TOP_K, tm), tok),
                   pl.BlockSpec((TOP_K, tm), tok),
                   pl.BlockSpec((E, LANES), const2)],
        out_shape=[jax.ShapeDtypeStruct((B, T, D), F32),
                   jax.ShapeDtypeStruct((B, T, D), BF16),
                   jax.ShapeDtypeStruct((B, T, D // 2), jnp.uint32),
                   jax.ShapeDtypeStruct((TOP_K, N), jnp.int32),
                   jax.ShapeDtypeStruct((TOP_K, N), F32),
                   jax.ShapeDtypeStruct((TOP_K, N), jnp.int32),
                   jax.ShapeDtypeStruct((E, LANES), F32)],
        scratch_shapes=[pltpu.VMEM((E, LANES), F32)],
        compiler_params=_cparams(2),
        name="oproj",
    )(o_nsa, o_gla, x, mod3, mod3, mod3, ln_g, ln_b, wo_pad, wr_hi, wr_lo, rbias, tri)


def _row_copy(src_ref, src_row, dst_ref, dst_row, sem):
    return pltpu.make_async_copy(src_ref.at[pl.ds(src_row, 1)], dst_ref.at[pl.ds(dst_row, 1)], sem)


def _dispatch_kernel(pstart_ref, pend_ref, eidx_ref, pos_ref, h2p_ref, xs_ref, zero_ref, sem, zsem):
    step = pl.program_id(0)
    tm = h2p_ref.shape[0]
    n_exp = pstart_ref.shape[0]

    @pl.when(step == 0)
    def _():
        zero_ref[...] = jnp.zeros_like(zero_ref)

        def zero_copy(e):
            return pltpu.make_async_copy(
                zero_ref, xs_ref.at[pl.ds(pl.multiple_of(pend_ref[e] - MOE_BLOCK, MOE_BLOCK), MOE_BLOCK)],
                zsem)

        def start(e, c):
            @pl.when(pend_ref[e] > pstart_ref[e])
            def _():
                zero_copy(e).start()
            return c

        def wait(e, c):
            @pl.when(pend_ref[e] > pstart_ref[e])
            def _():
                zero_copy(e).wait()
            return c

        lax.fori_loop(0, n_exp, start, 0)
        lax.fori_loop(0, n_exp, wait, 0)

    def start(t, c):
        for k in range(TOP_K):
            dst = pstart_ref[eidx_ref[k, t]] + pos_ref[k, t]
            _row_copy(h2p_ref, t, xs_ref, dst, sem).start()
        return c

    def wait(t, c):
        for k in range(TOP_K):
            _row_copy(h2p_ref, t, xs_ref, 0, sem).wait()
        return c

    lax.fori_loop(0, tm, start, 0)
    lax.fori_loop(0, tm, wait, 0)


def _dispatch_call(pstart, pend, eidx, pos, h2p, n_rows):
    N, W = h2p.shape
    tm = min(TM_DISP, N)
    grid_spec = pltpu.PrefetchScalarGridSpec(
        num_scalar_prefetch=2,
        grid=(N // tm,),
        in_specs=[pl.BlockSpec((TOP_K, tm), lambda i, *_: (0, i), memory_space=pltpu.SMEM),
                  pl.BlockSpec((TOP_K, tm), lambda i, *_: (0, i), memory_space=pltpu.SMEM),
                  pl.BlockSpec((tm, W), lambda i, *_: (i, 0))],
        out_specs=pl.BlockSpec(memory_space=pl.ANY),
        scratch_shapes=[pltpu.VMEM((MOE_BLOCK, W), jnp.uint32),
                        pltpu.SemaphoreType.DMA(()),
                        pltpu.SemaphoreType.DMA(())],
    )
    return pl.pallas_call(
        _dispatch_kernel,
        grid_spec=grid_spec,
        out_shape=jax.ShapeDtypeStruct((n_rows, W), jnp.uint32),
        compiler_params=_cparams(1),
        name="dispatch",
    )(pstart, pend, eidx, pos, h2p)


def _experts_kernel(blk_e_ref, nused_ref, xs_ref, wg_ref, wu_ref, wd_ref, ys_ref):
    @pl.when(pl.program_id(0) < nused_ref[0])
    def _():
        w = xs_ref[...]
        lo = pltpu.bitcast(w << 16, F32).astype(BF16)
        hi = pltpu.bitcast(w & jnp.uint32(0xFFFF0000), F32).astype(BF16)
        xb = jnp.concatenate([lo, hi], axis=1)
        hg = _dot(xb, wg_ref[0].astype(BF16))
        hu = _dot(xb, wu_ref[0].astype(BF16))
        hid = (_silu(hg) * hu).astype(BF16)
        ys_ref[...] = _dot(hid, wd_ref[0].astype(BF16))


def _experts_call(blk_e, nused, xs, w_gate, w_up, w_down):
    n_rows, W = xs.shape
    n_blocks = n_rows // MOE_BLOCK
    E, D, Hd = w_gate.shape
    blk = lambda i, be, nu: (jnp.minimum(i, nu[0] - 1), 0)
    wmap = lambda i, be, nu: (be[i], 0, 0)
    grid_spec = pltpu.PrefetchScalarGridSpec(
        num_scalar_prefetch=2,
        grid=(n_blocks,),
        in_specs=[pl.BlockSpec((MOE_BLOCK, W), blk),
                  pl.BlockSpec((1, D, Hd), wmap),
                  pl.BlockSpec((1, D, Hd), wmap),
                  pl.BlockSpec((1, Hd, D), wmap)],
        out_specs=pl.BlockSpec((MOE_BLOCK, D), blk),
    )
    return pl.pallas_call(
        _experts_kernel,
        grid_spec=grid_spec,
        out_shape=jax.ShapeDtypeStruct((n_rows, D), F32),
        compiler_params=_cparams(1),
        name="experts",
    )(blk_e, nused, xs, w_gate, w_up, w_down)


def _combine_kernel(pstart_ref, eidx_ref, pos_ref, ys_ref, wgt_ref, h2_ref, x1_ref, ga_ref,
                    g2_ref, b2_ref, wsg_ref, wsu_ref, wsd_ref, o_ref, buf_ref, sem, *, alpha):
    tm = h2_ref.shape[1]

    def start(t, c):
        for k in range(TOP_K):
            src = pstart_ref[eidx_ref[k, t]] + pos_ref[k, t]
            _row_copy(ys_ref, src, buf_ref.at[k], t, sem).start()
        return c

    def wait(t, c):
        for k in range(TOP_K):
            _row_copy(ys_ref, 0, buf_ref.at[k], t, sem).wait()
        return c

    lax.fori_loop(0, tm, start, 0)
    hb = h2_ref[0]
    hid = (_silu(_dot(hb, wsg_ref[...])) * _dot(hb, wsu_ref[...])).astype(BF16)
    y = _dot(hid, wsd_ref[...])
    lax.fori_loop(0, tm, wait, 0)
    wgt = wgt_ref[...]
    for k in range(TOP_K):
        y = y + buf_ref[k] * wgt[:, k:k + 1]
    o_ref[0] = _normalize_rows(alpha * x1_ref[0] + (1.0 + ga_ref[0]) * y) * g2_ref[...] + b2_ref[...]


def _combine_call(pstart, eidx, pos, ys, wgt_t, h2, x1, mod3, ln_g, ln_b, wsg, wsu, wsd, alpha):
    B, T, D = x1.shape
    tm = min(TM_COMB, T)
    nt = T // tm
    tokc = lambda b, i, *_: (0, b * nt + i)
    tokr = lambda b, i, *_: (b * nt + i, 0)
    row = lambda b, i, *_: (b, i, 0)
    const2 = lambda b, i, *_: (0, 0)
    grid_spec = pltpu.PrefetchScalarGridSpec(
        num_scalar_prefetch=1,
        grid=(B, nt),
        in_specs=[pl.BlockSpec((TOP_K, tm), tokc, memory_space=pltpu.SMEM),
                  pl.BlockSpec((TOP_K, tm), tokc, memory_space=pltpu.SMEM),
                  pl.BlockSpec(memory_space=pl.ANY),
                  pl.BlockSpec((tm, TOP_K), tokr),
                  pl.BlockSpec((1, tm, D), row),
                  pl.BlockSpec((1, tm, D), row),
                  pl.BlockSpec((1, 1, D), lambda b, i, *_: (b * 6 + 5, 0, 0)),
                  pl.BlockSpec((1, D), const2),
                  pl.BlockSpec((1, D), const2),
                  pl.BlockSpec(wsg.shape, const2),
                  pl.BlockSpec(wsu.shape, const2),
                  pl.BlockSpec(wsd.shape, const2)],
        out_specs=pl.BlockSpec((1, tm, D), row),
        scratch_shapes=[pltpu.VMEM((TOP_K, tm, D), F32),
                        pltpu.SemaphoreType.DMA(())],
    )
    return pl.pallas_call(
        functools.partial(_combine_kernel, alpha=alpha),
        grid_spec=grid_spec,
        out_shape=jax.ShapeDtypeStruct((B, T, D), F32),
        compiler_params=_cparams(2),
        name="combine",
    )(pstart, eidx, pos, ys, wgt_t, h2, x1, mod3, ln_g, ln_b, wsg, wsu, wsd)


def _pad_heads(w, n_heads, dh, side):
    D = w.shape[0]
    w = w.reshape(D, n_heads, dh)
    z = jnp.zeros_like(w)
    if side is None:
        out = jnp.concatenate([w, z], axis=-1)
    else:
        lo = jnp.concatenate([w, z], axis=-1)
        hi = jnp.concatenate([z, w], axis=-1)
        out = jnp.where(side[None, :, None], hi, lo)
    return out.reshape(D, n_heads * 2 * dh)


def _layout_w_in(w_in):
    D = w_in.shape[0]
    H, dh, G = NSA_HEADS, NSA_HEAD_DIM, NSA_KV_GROUPS
    o = 0
    wq = w_in[:, o:o + H * dh]; o += H * dh
    wcmp = w_in[:, o:o + 2 * G * dh]; o += 2 * G * dh
    wkvn = w_in[:, o:o + 4 * G * dh]; o += 4 * G * dh
    wng = w_in[:, o:o + H * 3]; o += H * 3
    wgq = w_in[:, o:o + GLA_HEADS * GLA_DK]; o += GLA_HEADS * GLA_DK
    wgk = w_in[:, o:o + GLA_HEADS * GLA_DK]; o += GLA_HEADS * GLA_DK
    wgv = w_in[:, o:o + GLA_HEADS * GLA_DV]; o += GLA_HEADS * GLA_DV
    wglr = w_in[:, o:o + GLA_GATE_RANK]; o += GLA_GATE_RANK
    wgr = w_in[:, o:o + GLA_HEADS * GLA_DV]; o += GLA_HEADS * GLA_DV
    high_half = (jnp.arange(H) // NSA_HPG) == 1
    wqn = _pad_heads(wq * (dh ** -0.5), H, dh, high_half)
    ng = wng.reshape(D, G, NSA_HPG * 3)
    wgate = jnp.concatenate([ng, jnp.zeros((D, G, LANES - NSA_HPG * 3), w_in.dtype)], -1).reshape(D, G * LANES)
    wgqk = jnp.concatenate([_pad_heads(wgq * (GLA_DK ** -0.5), GLA_HEADS, GLA_DK, None),
                            _pad_heads(wgk, GLA_HEADS, GLA_DK, None)], axis=1)
    wglr_p = jnp.concatenate([wglr, jnp.zeros((D, LANES - GLA_GATE_RANK), w_in.dtype)], axis=1)
    w_all = jnp.concatenate([wqn, wkvn, wcmp, wgate, wgqk, wgv, wglr_p, wgr], axis=1).astype(BF16)
    return w_all, wgv.T.astype(BF16)


def _overlap_matrix(n_cp, n_sp, seq_len):
    n_cmp = seq_len // CMP_STRIDE - CMP_BLOCK // CMP_STRIDE + 1
    n_slc = seq_len // SLC_BLOCK
    cs = np.arange(n_cmp) * CMP_STRIDE
    ce = cs + CMP_BLOCK - 1
    ss = np.arange(n_slc) * SLC_BLOCK
    se = ss + SLC_BLOCK - 1
    ov = np.clip(np.minimum(ce[:, None], se[None]) - np.maximum(cs[:, None], ss[None]) + 1, 0, None)
    out = np.zeros((n_cp, n_sp), np.float32)
    out[:n_cmp, :n_slc] = ov.astype(np.float32) / CMP_BLOCK
    return jnp.asarray(out, dtype=BF16)


def _round_up(a, m):
    return (a + m - 1) // m * m


def _token_mixer(x, mod3, w_in, cmp_pos_k, cmp_w1_k, cmp_w2_k, cmp_pos_v, cmp_w1_v, cmp_w2_v,
                 gla_w_a2, gla_b_a2, gla_norm_w):
    B, T, D = x.shape
    G, dh = NSA_KV_GROUPS, NSA_HEAD_DIM
    w_all, w_vt = _layout_w_in(w_in)
    qn, kvn, cmpkv, gates, gqk, gv, glr, gr, gvt = _inproj_call(x, mod3, w_all, w_vt)

    n_sub = T // CMP_STRIDE
    sub = cmpkv.reshape(B, n_sub, CMP_STRIDE, 2, G, dh).transpose(3, 0, 4, 1, 2, 5)
    sub = sub.reshape(2, B, G, n_sub, CMP_STRIDE * dh)
    pos2 = jnp.stack([cmp_pos_k, cmp_pos_v]).reshape(2, CMP_BLOCK // CMP_STRIDE, CMP_STRIDE * dh)
    w1 = jnp.stack([cmp_w1_k, cmp_w1_v])
    w2 = jnp.stack([cmp_w2_k, cmp_w2_v])
    cmp_out = _compress_call(sub, pos2, w1, w2)
    n_cp = _round_up(n_sub, LANES)
    kcvc = cmp_out.transpose(0, 1, 3, 2, 4).reshape(2, B, n_sub, G * dh).astype(BF16)
    kcvc = jnp.pad(kcvc, ((0, 0), (0, 0), (0, n_cp - n_sub), (0, 0)))
    n_sp = _round_up(T // SLC_BLOCK, LANES)
    o_nsa = _nsa_call(qn, gates, kcvc[0], kcvc[1], kvn, _overlap_matrix(n_cp, n_sp, T))

    wa = gla_w_a2.reshape(GLA_GATE_RANK, GLA_HEADS, GLA_DK)
    wa_pad = jnp.zeros((LANES, GLA_HEADS, LANES), F32).at[:GLA_GATE_RANK, :, :GLA_DK].set(wa)
    wa_pad = wa_pad.reshape(LANES, GLA_HEADS * LANES)
    ba_pad = jnp.zeros((GLA_HEADS, LANES), F32).at[:, :GLA_DK].set(gla_b_a2.reshape(GLA_HEADS, GLA_DK))
    ba_pad = ba_pad.reshape(1, GLA_HEADS * LANES)
    o_gla = _gla_call(gqk, gv, gvt, glr, gr, wa_pad, ba_pad, gla_norm_w.reshape(1, GLA_DV))
    return o_nsa, o_gla


def _layer(x, c_pad, l, alpha, w_mod, b_mod, w_in, cmp_pos_k, cmp_w1_k, cmp_w2_k, cmp_pos_v, cmp_w1_v,
           cmp_w2_v, gla_w_a2, gla_b_a2, gla_norm_w, w_o, ln1_g, ln1_b, w_router, router_bias,
           w_e_gate, w_e_up, w_e_down, w_s_gate, w_s_up, w_s_down, ln2_g, ln2_b):
    B, T, D = x.shape
    N = B * T
    E = N_EXPERTS
    mod = _mod_call(c_pad, w_mod[l], b_mod[l].reshape(1, -1))
    mod3 = mod[:B].reshape(B * 6, 1, D)

    o_nsa, o_gla = _token_mixer(x, mod3, w_in[l], cmp_pos_k[l], cmp_w1_k[l], cmp_w2_k[l], cmp_pos_v[l],
                                cmp_w1_v[l], cmp_w2_v[l], gla_w_a2[l], gla_b_a2[l], gla_norm_w[l])

    wo = w_o[l]
    n_nsa = NSA_HEADS * NSA_HEAD_DIM
    high_half = (jnp.arange(NSA_HEADS) // NSA_HPG) == 1
    wo_nsa = _pad_heads(wo[:n_nsa].T, NSA_HEADS, NSA_HEAD_DIM, high_half).T
    wo_pad = jnp.concatenate([wo_nsa, wo[n_nsa:]], axis=0).astype(BF16)
    wr_t = w_router[l].T
    wr_hi = wr_t.astype(BF16)
    wr_lo = (wr_t - wr_hi.astype(F32)).astype(BF16)
    tm_o = min(TM_OPROJ, T)
    tri = jnp.asarray(np.triu(np.ones((tm_o, tm_o), np.float32), 1), dtype=BF16)
    x1, h2, h2p, eidx, wgt, pos, counts = _oproj_call(
        o_nsa, o_gla, x, mod3, ln1_g[l].reshape(1, D), ln1_b[l].reshape(1, D), wo_pad, wr_hi, wr_lo,
        router_bias[l].reshape(E, 1), tri, alpha)

    cnt = counts[:, 0].astype(jnp.int32)
    padded = (cnt + MOE_BLOCK - 1) // MOE_BLOCK * MOE_BLOCK
    pend = jnp.cumsum(padded)
    pstart = pend - padded
    n_blocks = -(-N * TOP_K // MOE_BLOCK) + E
    blk_start = jnp.arange(n_blocks, dtype=jnp.int32) * MOE_BLOCK
    nused = (pend[-1] // MOE_BLOCK).astype(jnp.int32)
    blk_e = jnp.searchsorted(pend, jnp.minimum(blk_start, pend[-1] - 1), side='right').astype(jnp.int32)
    blk_e = jnp.minimum(blk_e, E - 1)

    xs = _dispatch_call(pstart, pend, eidx, pos, h2p.reshape(N, D // 2), n_blocks * MOE_BLOCK)
    ys = _experts_call(blk_e, nused.reshape(1), xs, w_e_gate[l], w_e_up[l], w_e_down[l])
    return _combine_call(pstart, eidx, pos, ys, wgt.T, h2, x1, mod3, ln2_g[l].reshape(1, D),
                         ln2_b[l].reshape(1, D), w_s_gate[l].astype(BF16), w_s_up[l].astype(BF16),
                         w_s_down[l].astype(BF16), alpha)


def kernel(x, c, w_mod, b_mod, w_in, cmp_pos_k, cmp_w1_k, cmp_w2_k, cmp_pos_v, cmp_w1_v, cmp_w2_v, gla_w_a2, gla_b_a2, gla_norm_w, w_o, ln1_g, ln1_b, w_router, router_bias, w_e_gate, w_e_up, w_e_down, w_s_gate, w_s_up, w_s_down, ln2_g, ln2_b):
    depth = w_mod.shape[0]
    alpha = (2.0 * depth) ** 0.25
    B = x.shape[0]
    c_pad = jnp.pad(c, ((0, (-B) % 8), (0, 0)))
    for l in range(depth):
        x = _layer(x, c_pad, l, alpha, w_mod, b_mod, w_in, cmp_pos_k, cmp_w1_k, cmp_w2_k, cmp_pos_v,
                   cmp_w1_v, cmp_w2_v, gla_w_a2, gla_b_a2, gla_norm_w, w_o, ln1_g, ln1_b, w_router,
                   router_bias, w_e_gate, w_e_up, w_e_down, w_s_gate, w_s_up, w_s_down, ln2_g, ln2_b)
    return x
```

```python
import functools

import jax
import jax.numpy as jnp
import numpy as np
from jax import lax
from jax.experimental import pallas as pl
from jax.experimental.pallas import tpu as pltpu

NSA_HEADS = 8
NSA_KV_GROUPS = 2
NSA_HPG = NSA_HEADS // NSA_KV_GROUPS
NSA_HEAD_DIM = 64
CMP_BLOCK = 32
CMP_STRIDE = 16
CMP_HIDDEN = 256
SLC_BLOCK = 64
SLC_TOPN = 16
WINDOW = 512
FORCE_BONUS = 1e4
GLA_HEADS = 4
GLA_DK = 64
GLA_DV = 128
GLA_GATE_RANK = 16
GLA_TAU = 16.0
N_EXPERTS = 256
TOP_K = 8
N_EXPERT_GROUPS = 8
TOPK_GROUPS = 4
EXPERT_HIDDEN = 256
ROUTE_SCALE = 2.5
LN_EPS = 1e-5
NEG = -1e30
TAKEN = -3e38

LANES = 128
VMEM_LIMIT = 56 * 1024 * 1024
N_DMA_QUEUES = 2

TM_PROJ = 256
TQ_NSA = 256
TK_SLC = 512
GLA_CHUNK = 128
GLA_SUB = 16
GLA_EXP_CLAMP = 80.0
TM_OPROJ = 256
MOE_BLOCK = 128
TM_DISP = 128
TM_COMB = 128

BF16 = jnp.bfloat16
F32 = jnp.float32


def _cparams(n_axes):
    return pltpu.CompilerParams(
        dimension_semantics=("arbitrary",) * n_axes, vmem_limit_bytes=VMEM_LIMIT)


def _dot(a, b):
    return jnp.dot(a, b, preferred_element_type=F32)


def _dot_nt(a, b):
    return lax.dot_general(a, b, (((1,), (1,)), ((), ())), preferred_element_type=F32)


def _split(a):
    hi = a.astype(BF16)
    lo = (a - hi.astype(F32)).astype(BF16)
    return hi, lo


def _dot3(a, b):
    ah, al = _split(a)
    bh, bl = _split(b)
    return _dot(ah, bh) + _dot(ah, bl) + _dot(al, bh)


def _silu(x):
    return x * (1.0 / (1.0 + jnp.exp(-x)))


def _sigmoid(x):
    return 1.0 / (1.0 + jnp.exp(-x))


def _normalize_rows(x):
    mu = jnp.mean(x, axis=-1, keepdims=True)
    xc = x - mu
    var = jnp.mean(xc * xc, axis=-1, keepdims=True)
    return xc * lax.rsqrt(var + LN_EPS)


def _mod_kernel(c_ref, w_ref, b_ref, o_ref):
    o_ref[...] = _dot3(_silu(c_ref[...]), w_ref[...]) + b_ref[...]


def _mod_call(c_pad, w_mod, b_mod):
    rows, d = c_pad.shape
    n_out = w_mod.shape[1]
    return pl.pallas_call(
        _mod_kernel,
        grid=(n_out // d,),
        in_specs=[pl.BlockSpec((rows, d), lambda j: (0, 0)),
                  pl.BlockSpec((d, d), lambda j: (0, j)),
                  pl.BlockSpec((1, d), lambda j: (0, j))],
        out_specs=pl.BlockSpec((rows, d), lambda j: (0, j)),
        out_shape=jax.ShapeDtypeStruct((rows, n_out), F32),
        compiler_params=_cparams(1),
        name="mod",
    )(c_pad, w_mod, b_mod)


_PROJ_COLS = (("qn", 1024, BF16), ("kvn", 512, BF16), ("cmp", 256, F32), ("gate", 256, F32),
              ("gqk", 1024, F32), ("gv", 512, BF16), ("glr", 128, F32), ("gr", 512, F32))


def _inproj_kernel(x_ref, sh_ref, sc_ref, w_ref, wvt_ref,
                   qn_ref, kvn_ref, cmp_ref, gate_ref, gqk_ref, gv_ref, glr_ref, gr_ref, gvt_ref):
    h = _normalize_rows(x_ref[0]) * (1.0 + sc_ref[0]) + sh_ref[0]
    hb = h.astype(BF16)
    outs = (qn_ref, kvn_ref, cmp_ref, gate_ref, gqk_ref, gv_ref, glr_ref, gr_ref)
    c0 = 0
    for (name, width, dt), o_ref in zip(_PROJ_COLS, outs):
        p = _dot(hb, w_ref[:, c0:c0 + width])
        if name == "gate":
            p = _sigmoid(p)
        o_ref[0] = p.astype(dt)
        c0 += width
    gvt_ref[0] = _dot_nt(wvt_ref[...], hb).astype(BF16)


def _inproj_call(x, mod3, w_all, w_vt):
    B, T, D = x.shape
    tm = min(TM_PROJ, T)
    ctot = w_all.shape[1]
    out_shape = [jax.ShapeDtypeStruct((B, T, wdt), dt) for _, wdt, dt in _PROJ_COLS]
    out_shape.append(jax.ShapeDtypeStruct((B, w_vt.shape[0], T), BF16))
    out_specs = [pl.BlockSpec((1, tm, wdt), lambda b, i: (b, i, 0)) for _, wdt, _ in _PROJ_COLS]
    out_specs.append(pl.BlockSpec((1, w_vt.shape[0], tm), lambda b, i: (b, 0, i)))
    return pl.pallas_call(
        _inproj_kernel,
        grid=(B, T // tm),
        in_specs=[pl.BlockSpec((1, tm, D), lambda b, i: (b, i, 0)),
                  pl.BlockSpec((1, 1, D), lambda b, i: (b * 6 + 0, 0, 0)),
                  pl.BlockSpec((1, 1, D), lambda b, i: (b * 6 + 1, 0, 0)),
                  pl.BlockSpec((D, ctot), lambda b, i: (0, 0)),
                  pl.BlockSpec(w_vt.shape, lambda b, i: (0, 0))],
        out_specs=out_specs,
        out_shape=out_shape,
        compiler_params=_cparams(2),
        name="inproj",
    )(x, mod3, mod3, w_all, w_vt)


def _compress_kernel(x_ref, pos_ref, w1_ref, w2_ref, o_ref):
    G, dh, st = NSA_KV_GROUPS, NSA_HEAD_DIM, CMP_STRIDE
    n_sub = o_ref.shape[2]
    zero_w = jnp.zeros((dh, w1_ref.shape[2]), BF16)
    out = jnp.zeros((n_sub, G * dh), F32)
    for g in range(G):
        halves = [jnp.zeros((n_sub, w1_ref.shape[2]), F32) for _ in range(CMP_BLOCK // st)]
        for l in range(st):
            x_l = x_ref[0, pl.ds(l, n_sub, stride=st), :]
            for half in range(len(halves)):
                r = half * st + l
                a = (x_l + pos_ref[0, r:r + 1, :]).astype(BF16)
                w = w1_ref[0, r * dh:(r + 1) * dh, :].astype(BF16)
                w = jnp.concatenate([w, zero_w] if g == 0 else [zero_w, w], axis=0)
                halves[half] = halves[half] + _dot(a, w)
        hid = halves[0] + pltpu.roll(halves[1], n_sub - 1, 0)
        out = out + _dot(_silu(hid).astype(BF16), w2_ref[0, g].astype(BF16))
    o_ref[0, 0] = out.astype(o_ref.dtype)


def _compress_call(cmpkv, pos2, w1, w2p):
    B, T, _ = cmpkv.shape
    n_sub = T // CMP_STRIDE
    return pl.pallas_call(
        _compress_kernel,
        grid=(2, B),
        in_specs=[pl.BlockSpec((1, T, LANES), lambda s, b: (b, 0, s)),
                  pl.BlockSpec((1,) + pos2.shape[1:], lambda s, b: (s, 0, 0)),
                  pl.BlockSpec((1,) + w1.shape[1:], lambda s, b: (s, 0, 0)),
                  pl.BlockSpec((1,) + w2p.shape[1:], lambda s, b: (s, 0, 0, 0))],
        out_specs=pl.BlockSpec((1, 1, n_sub, LANES), lambda s, b: (s, b, 0, 0)),
        out_shape=jax.ShapeDtypeStruct((2, B, n_sub, LANES), BF16),
        compiler_params=_cparams(2),
        name="compress",
    )(cmpkv, pos2, w1, w2p)


def _position_features(pos, n_slc):
    assert n_slc + 2 <= LANES and pos.max() // SLC_BLOCK < 256
    f = np.zeros((pos.shape[0], LANES), np.float32)
    blk = pos // SLC_BLOCK
    onehot = blk < n_slc
    f[np.arange(pos.shape[0])[onehot], blk[onehot]] = 1.0
    f[:, n_slc] = blk
    f[:, n_slc + 1] = pos % SLC_BLOCK
    return f


def _nsa_kernel(q_ref, gate_ref, kc_ref, vc_ref, kv_ref, kf_ref, ov_ref, o_ref,
                st_ref, kx_ref, kwx_ref, *, seq_len):
    g = pl.program_id(1)
    qi = pl.program_id(2)
    tq = q_ref.shape[1]
    P = NSA_HPG
    R = P * tq
    n_cp = kc_ref.shape[1]
    n_sp = ov_ref.shape[1]
    n_slc = seq_len // SLC_BLOCK
    n_sel = min(SLC_TOPN, n_slc)
    q0 = qi * tq

    @pl.when((g == 0) & (qi == 0))
    def _():
        kx_ref[:, 0:LANES] = kv_ref[0, :, 0:LANES]
        kx_ref[:, LANES:2 * LANES] = kf_ref[...]
        kwx_ref[:, 0:LANES] = kv_ref[0, :, 2 * LANES:3 * LANES]
        kwx_ref[:, LANES:2 * LANES] = kf_ref[...]

    qblk = q_ref[0]
    gscale = jnp.where(g == 0, 1.0, 2.0 ** (-P)).astype(F32)
    lane = lax.broadcasted_iota(jnp.int32, (1, LANES), 1)
    alibi_lanes = jnp.where(lane == n_slc, float(SLC_BLOCK), jnp.where(lane == n_slc + 1, 1.0, 0.0))
    q_heads = [qblk[:, p * LANES:(p + 1) * LANES] for p in range(P)]
    f_pos = [jnp.broadcast_to(alibi_lanes * (2.0 ** (-(p + 1)) * gscale), (tq, LANES)) for p in range(P)]
    Qa = jnp.concatenate([jnp.concatenate([q_heads[p], f_pos[p].astype(BF16)], axis=1)
                          for p in range(P)], axis=0)
    t_q = q0 + lax.broadcasted_iota(jnp.int32, (tq, 1), 0)
    t_row = jnp.concatenate([t_q] * P, axis=0)

    def heads(x):
        return jnp.concatenate([x] * P, axis=0)

    n_idx = lax.broadcasted_iota(jnp.int32, (1, n_cp), 1)
    ce = n_idx * CMP_STRIDE + (CMP_BLOCK - 1)
    s_c = _dot_nt(Qa, kc_ref[0]) + heads(jnp.where(ce <= t_q, 0.0, NEG))
    e_c = jnp.exp(s_c - jnp.max(s_c, axis=-1, keepdims=True))
    inv_c = jnp.where(t_row >= CMP_BLOCK - 1,
                      1.0 / jnp.maximum(jnp.sum(e_c, axis=-1, keepdims=True), 1e-30), 0.0)
    p_c = e_c * inv_c
    o_cmp = _dot(p_c.astype(BF16), vc_ref[0])

    p_sum = p_c[0:tq]
    for p in range(1, P):
        p_sum = p_sum + p_c[p * tq:(p + 1) * tq]
    ph, plo = _split(p_sum)
    imp = _dot(ph, ov_ref[...]) + _dot(plo, ov_ref[...])
    j_idx = lax.broadcasted_iota(jnp.int32, (1, n_sp), 1)
    t_blk = t_q // SLC_BLOCK
    causal_blk = j_idx <= t_blk
    forced = (j_idx == 0) | (j_idx == t_blk) | (j_idx == t_blk - 1)
    score = jnp.where(causal_blk, imp + jnp.where(forced, FORCE_BONUS, 0.0), NEG)
    st_ref[...] = score.T
    n_rk = _round_up(n_slc, 8)
    s_all = st_ref[0:n_rk, :]
    jrow = lax.broadcasted_iota(jnp.int32, (n_rk, tq), 0)

    def rank_body(jp, cnt):
        row = st_ref[pl.ds(jp, 1), :]
        ge = jnp.where(row >= s_all, 1.0, 0.0)
        gt = jnp.where(row > s_all, 1.0, 0.0)
        return cnt + jnp.where(jrow > jp, ge, gt)

    n_causal = (q0 + tq) // SLC_BLOCK
    cnt = lax.fori_loop(0, n_causal, rank_body, jnp.zeros((n_rk, tq), F32))
    unsel = jnp.where(cnt < n_sel, 0.0, NEG)
    if n_rk < n_sp:
        unsel = jnp.concatenate([unsel, jnp.zeros((n_sp - n_rk, tq), F32)], axis=0)
    sel_bias = unsel.T

    tk = min(TK_SLC, seq_len)
    Qs = jnp.concatenate([jnp.concatenate([q_heads[p], (sel_bias + f_pos[p]).astype(BF16)], axis=1)
                          for p in range(P)], axis=0)

    def slc_tile(k0, carry, bias):
        m_i, l_i, acc = carry
        s = _dot_nt(Qs, kx_ref[pl.ds(k0, tk), :])
        if bias is not None:
            s = s + bias
        m_new = jnp.maximum(m_i, jnp.max(s, axis=-1, keepdims=True))
        e = jnp.exp(s - m_new)
        alpha = jnp.exp(m_i - m_new)
        l_new = alpha * l_i + jnp.sum(e, axis=-1, keepdims=True)
        acc_new = alpha * acc + _dot(e.astype(BF16), kv_ref[0, pl.ds(k0, tk), LANES:2 * LANES])
        return m_new, l_new, acc_new

    n_full = q0 // tk
    init = (jnp.full((R, 1), NEG, F32), jnp.zeros((R, 1), F32), jnp.zeros((R, LANES), F32))
    carry = lax.fori_loop(0, n_full, lambda it, c: slc_tile(pl.multiple_of(it * tk, tk), c, None), init)
    k_last = pl.multiple_of(n_full * tk, tk)
    kpos = k_last + lax.broadcasted_iota(jnp.int32, (1, tk), 1)
    _, l_s, acc_s = slc_tile(k_last, carry, heads(jnp.where(kpos <= t_q, 0.0, NEG)))
    o_slc = acc_s / jnp.maximum(l_s, 1e-30)

    wlen = min(WINDOW + tq, seq_len)
    w0 = pl.multiple_of(jnp.minimum(jnp.maximum(q0 - WINDOW, 0), seq_len - wlen), tq)
    wpos = w0 + lax.broadcasted_iota(jnp.int32, (1, wlen), 1)
    win_bias = jnp.where(wpos <= t_q, jnp.where(wpos > t_q - WINDOW, 0.0, NEG), NEG)
    s_w = _dot_nt(Qa, kwx_ref[pl.ds(w0, wlen), :]) + heads(win_bias)
    e_w = jnp.exp(s_w - jnp.max(s_w, axis=-1, keepdims=True))
    o_win = _dot(e_w.astype(BF16), kv_ref[0, pl.ds(w0, wlen), 3 * LANES:4 * LANES])
    o_win = o_win / jnp.maximum(jnp.sum(e_w, axis=-1, keepdims=True), 1e-30)

    gates = gate_ref[0]
    lane = lax.broadcasted_iota(jnp.int32, (1, LANES), 1)
    live = (lane // NSA_HEAD_DIM) == g
    pieces = []
    for p in range(P):
        rows = slice(p * tq, (p + 1) * tq)
        o_p = (gates[:, 3 * p + 0:3 * p + 1] * o_cmp[rows]
               + gates[:, 3 * p + 1:3 * p + 2] * o_slc[rows]
               + gates[:, 3 * p + 2:3 * p + 3] * o_win[rows])
        pieces.append(jnp.where(live, o_p, 0.0))
    o_ref[0] = jnp.concatenate(pieces, axis=1).astype(o_ref.dtype)


def _nsa_call(qn, gates, kcx, vc, kvn, kfeat, overlap):
    B, T, _ = qn.shape
    G, P = NSA_KV_GROUPS, NSA_HPG
    tq = min(TQ_NSA, T)
    n_cp = kcx.shape[1]
    n_sp = overlap.shape[1]
    assert (T // SLC_BLOCK) % 8 == 0 and min(TK_SLC, T) % tq == 0
    return pl.pallas_call(
        functools.partial(_nsa_kernel, seq_len=T),
        grid=(B, G, T // tq),
        in_specs=[pl.BlockSpec((1, tq, P * LANES), lambda b, g, i: (b, i, g)),
                  pl.BlockSpec((1, tq, LANES), lambda b, g, i: (b, i, g)),
                  pl.BlockSpec((1, n_cp, 2 * LANES), lambda b, g, i: (b, 0, 0)),
                  pl.BlockSpec((1, n_cp, LANES), lambda b, g, i: (b, 0, 0)),
                  pl.BlockSpec((1, T, 4 * LANES), lambda b, g, i: (b, 0, 0)),
                  pl.BlockSpec((T, LANES), lambda b, g, i: (0, 0)),
                  pl.BlockSpec((n_cp, n_sp), lambda b, g, i: (0, 0))],
        out_specs=pl.BlockSpec((1, tq, P * LANES), lambda b, g, i: (b, i, g)),
        out_shape=jax.ShapeDtypeStruct((B, T, G * P * LANES), BF16),
        scratch_shapes=[pltpu.VMEM((n_sp, tq), F32),
                        pltpu.VMEM((T, 2 * LANES), BF16),
                        pltpu.VMEM((T, 2 * LANES), BF16)],
        compiler_params=_cparams(3),
        name="nsa",
    )(qn, gates, kcx, vc, kvn, kfeat, overlap)


def _gla_kernel(qk_ref, v_ref, vt_ref, glr_ref, r_ref, wa_ref, ba_ref, nw_ref, o_ref, st_ref):
    ci = pl.program_id(1)
    C = qk_ref.shape[1]
    H = GLA_HEADS
    n_sub = C // GLA_SUB

    @pl.when(ci == 0)
    def _():
        st_ref[...] = jnp.zeros_like(st_ref)

    z = _dot3(glr_ref[0], wa_ref[...]) + ba_ref[...]
    gdec = -(jnp.maximum(-z, 0.0) + jnp.log(1.0 + jnp.exp(-jnp.abs(z)))) * (1.0 / GLA_TAU)
    rowi = lax.broadcasted_iota(jnp.int32, (C, 1), 0)
    b = gdec
    sh = 1
    while sh < C:
        b = b + jnp.where(rowi >= sh, pltpu.roll(b, sh, 0), 0.0)
        sh *= 2

    qk = qk_ref[0]
    r_i = lax.broadcasted_iota(jnp.int32, (C, C), 0)
    c_i = lax.broadcasted_iota(jnp.int32, (C, C), 1)
    causal = c_i <= r_i
    for h in range(H):
        cols = slice(h * LANES, (h + 1) * LANES)
        qh = qk[:, h * LANES:(h + 1) * LANES]
        kh = qk[:, (H + h) * LANES:(H + h + 1) * LANES]
        bh = b[:, cols]
        bl = bh[C - 1:C, :]
        st = st_ref[h]
        o_inter = _dot_nt((qh * jnp.exp(bh)).astype(BF16), st.astype(BF16))
        a_rows = []
        for i in range(n_sub):
            rs = slice(i * GLA_SUB, (i + 1) * GLA_SUB)
            ref_b = bh[i * GLA_SUB:i * GLA_SUB + 1, :]
            q_i = (qh[rs] * jnp.exp(bh[rs] - ref_b)).astype(BF16)
            k_i = (kh * jnp.exp(jnp.minimum(ref_b - bh, GLA_EXP_CLAMP))).astype(BF16)
            a_rows.append(_dot_nt(q_i, k_i))
        attn = jnp.where(causal, jnp.concatenate(a_rows, axis=0), 0.0)
        o_h = o_inter + _dot(attn.astype(BF16), v_ref[0, :, cols])
        kd = (kh * jnp.exp(bl - bh)).astype(BF16)
        st_ref[h] = st * jnp.exp(bl) + _dot(vt_ref[0, h * GLA_DV:(h + 1) * GLA_DV, :], kd)
        o_n = _normalize_rows(o_h) * nw_ref[...]
        o_ref[0, :, cols] = (o_n * _silu(r_ref[0, :, cols])).astype(o_ref.dtype)


def _gla_call(gqk, gv, gvt, glr, gr, wa_pad, ba_pad, norm_w):
    B, T, _ = gqk.shape
    H = GLA_HEADS
    C = min(GLA_CHUNK, T)
    return pl.pallas_call(
        _gla_kernel,
        grid=(B, T // C),
        in_specs=[pl.BlockSpec((1, C, 2 * H * LANES), lambda b, i: (b, i, 0)),
                  pl.BlockSpec((1, C, H * GLA_DV), lambda b, i: (b, i, 0)),
                  pl.BlockSpec((1, H * GLA_DV, C), lambda b, i: (b, 0, i)),
                  pl.BlockSpec((1, C, LANES), lambda b, i: (b, i, 0)),
                  pl.BlockSpec((1, C, H * GLA_DV), lambda b, i: (b, i, 0)),
                  pl.BlockSpec(wa_pad.shape, lambda b, i: (0, 0)),
                  pl.BlockSpec(ba_pad.shape, lambda b, i: (0, 0)),
                  pl.BlockSpec(norm_w.shape, lambda b, i: (0, 0))],
        out_specs=pl.BlockSpec((1, C, H * GLA_DV), lambda b, i: (b, i, 0)),
        out_shape=jax.ShapeDtypeStruct((B, T, H * GLA_DV), BF16),
        scratch_shapes=[pltpu.VMEM((H, GLA_DV, LANES), F32)],
        compiler_params=_cparams(2),
        name="gla",
    )(gqk, gv, gvt, glr, gr, wa_pad, ba_pad, norm_w)


def _oproj_kernel(on_ref, og_ref, x_ref, ga_ref, sh_ref, sc_ref, g1_ref, b1_ref, wo_ref,
                  wrh_ref, wrl_ref, rb_ref, tri_ref,
                  x1_ref, h2_ref, h2p_ref, eidx_ref, wgt_ref, pos_ref, cnt_ref, run_ref, *, alpha):
    first = (pl.program_id(0) == 0) & (pl.program_id(1) == 0)

    @pl.when(first)
    def _():
        run_ref[...] = jnp.zeros_like(run_ref)

    kn = on_ref.shape[2]
    y = _dot(on_ref[0], wo_ref[0:kn, :]) + _dot(og_ref[0], wo_ref[kn:, :])
    x1 = _normalize_rows(alpha * x_ref[0] + (1.0 + ga_ref[0]) * y) * g1_ref[...] + b1_ref[...]
    x1_ref[0] = x1
    h2 = _normalize_rows(x1) * (1.0 + sc_ref[0]) + sh_ref[0]
    hh, hl = _split(h2)
    h2_ref[0] = hh
    half = h2.shape[1] // 2
    bits = pltpu.bitcast(hh.astype(F32), jnp.uint32)
    h2p_ref[0] = (bits[:, 0:half] >> 16) | (bits[:, half:] & jnp.uint32(0xFFFF0000))

    logit = _dot_nt(wrh_ref[...], hh) + _dot_nt(wrh_ref[...], hl) + _dot_nt(wrl_ref[...], hh)
    s = _sigmoid(logit)
    sb = s + rb_ref[...]
    E, tm = s.shape
    gsz = E // N_EXPERT_GROUPS
    gi = lax.broadcasted_iota(jnp.int32, (gsz, tm), 0).astype(F32)
    gscore = []
    for gidx in range(N_EXPERT_GROUPS):
        blk = sb[gidx * gsz:(gidx + 1) * gsz]
        m1 = jnp.max(blk, axis=0, keepdims=True)
        i1 = jnp.min(jnp.where(blk == m1, gi, float(gsz)), axis=0, keepdims=True)
        m2 = jnp.max(jnp.where(gi == i1, TAKEN, blk), axis=0, keepdims=True)
        gscore.append(m1 + m2)
    masked = []
    for gidx in range(N_EXPERT_GROUPS):
        rank = jnp.zeros((1, tm), F32)
        for other in range(N_EXPERT_GROUPS):
            if other == gidx:
                continue
            beats = (gscore[other] >= gscore[gidx]) if other < gidx else (gscore[other] > gscore[gidx])
            rank = rank + jnp.where(beats, 1.0, 0.0)
        masked.append(jnp.where(rank < TOPK_GROUPS, sb[gidx * gsz:(gidx + 1) * gsz], NEG))
    ms = jnp.concatenate(masked, axis=0)
    ei = lax.broadcasted_iota(jnp.int32, (E, tm), 0).astype(F32)
    picks, idxs, vals = [], [], []
    for _ in range(TOP_K):
        m = jnp.max(ms, axis=0, keepdims=True)
        ix = jnp.min(jnp.where(ms == m, ei, float(E)), axis=0, keepdims=True)
        pick = ei == ix
        ms = jnp.where(pick, TAKEN, ms)
        picks.append(pick)
        idxs.append(ix)
        vals.append(jnp.sum(jnp.where(pick, s, 0.0), axis=0, keepdims=True))
    wsum = vals[0]
    for v in vals[1:]:
        wsum = wsum + v
    sel = jnp.zeros((E, tm), F32)
    for pick in picks:
        sel = sel + jnp.where(pick, 1.0, 0.0)
    pos_all = run_ref[:, 0:1] + _dot(sel.astype(BF16), tri_ref[...])
    run_ref[...] = run_ref[...] + jnp.sum(sel, axis=1, keepdims=True)
    cnt_ref[...] = run_ref[...]
    for k in range(TOP_K):
        eidx_ref[k:k + 1, :] = idxs[k].astype(jnp.int32)
        wgt_ref[k:k + 1, :] = vals[k] / wsum * ROUTE_SCALE
        pos_ref[k:k + 1, :] = jnp.sum(jnp.where(picks[k], pos_all, 0.0), axis=0,
                                      keepdims=True).astype(jnp.int32)


def _oproj_call(o_nsa, o_gla, x, mod3, ln_g, ln_b, wo_pad, wr_hi, wr_lo, rbias, tri, alpha):
    B, T, D = x.shape
    tm = min(TM_OPROJ, T)
    N = B * T
    E = wr_hi.shape[0]
    nt = T // tm
    tok = lambda b, i: (0, b * nt + i)
    const2 = lambda b, i: (0, 0)
    return pl.pallas_call(
        functools.partial(_oproj_kernel, alpha=alpha),
        grid=(B, nt),
        in_specs=[pl.BlockSpec((1, tm, o_nsa.shape[2]), lambda b, i: (b, i, 0)),
                  pl.BlockSpec((1, tm, o_gla.shape[2]), lambda b, i: (b, i, 0)),
                  pl.BlockSpec((1, tm, D), lambda b, i: (b, i, 0)),
                  pl.BlockSpec((1, 1, D), lambda b, i: (b * 6 + 2, 0, 0)),
                  pl.BlockSpec((1, 1, D), lambda b, i: (b * 6 + 3, 0, 0)),
                  pl.BlockSpec((1, 1, D), lambda b, i: (b * 6 + 4, 0, 0)),
                  pl.BlockSpec((1, D), const2),
                  pl.BlockSpec((1, D), const2),
                  pl.BlockSpec(wo_pad.shape, const2),
                  pl.BlockSpec(wr_hi.shape, const2),
                  pl.BlockSpec(wr_lo.shape, const2),
                  pl.BlockSpec(rbias.shape, const2),
                  pl.BlockSpec(tri.shape, const2)],
        out_specs=[pl.BlockSpec((1, tm, D), lambda b, i: (b, i, 0)),
                   pl.BlockSpec((1, tm, D), lambda b, i: (b, i, 0)),
                   pl.BlockSpec((1, tm, D // 2), lambda b, i: (b, i, 0)),
                   pl.BlockSpec((TOP_K, tm), tok),
                   pl.BlockSpec((TOP_K, tm), tok),
                   pl.BlockSpec((TOP_K, tm), tok),
                   pl.BlockSpec((E, LANES), const2)],
        out_shape=[jax.ShapeDtypeStruct((B, T, D), F32),
                   jax.ShapeDtypeStruct((B, T, D), BF16),
                   jax.ShapeDtypeStruct((B, T, D // 2), jnp.uint32),
                   jax.ShapeDtypeStruct((TOP_K, N), jnp.int32),
                   jax.ShapeDtypeStruct((TOP_K, N), F32),
                   jax.ShapeDtypeStruct((TOP_K, N), jnp.int32),
                   jax.ShapeDtypeStruct((E, LANES), F32)],
        scratch_shapes=[pltpu.VMEM((E, LANES), F32)],
        compiler_params=_cparams(2),
        name="oproj",
    )(o_nsa, o_gla, x, mod3, mod3, mod3, ln_g, ln_b, wo_pad, wr_hi, wr_lo, rbias, tri)


def _row_copy(src_ref, src_row, dst_ref, dst_row, sem):
    return pltpu.make_async_copy(src_ref.at[pl.ds(src_row, 1)], dst_ref.at[pl.ds(dst_row, 1)], sem)


def _dispatch_kernel(pstart_ref, pend_ref, eidx_ref, pos_ref, h2p_ref, xs_ref, zero_ref, sem, zsem):
    step = pl.program_id(0)
    tm = h2p_ref.shape[0]
    n_exp = pstart_ref.shape[0]

    @pl.when(step == 0)
    def _():
        zero_ref[...] = jnp.zeros_like(zero_ref)
        n_tail = xs_ref.shape[0] // MOE_BLOCK - pend_ref[n_exp - 1] // MOE_BLOCK

        def zero_block(row0):
            return pltpu.make_async_copy(
                zero_ref, xs_ref.at[pl.ds(pl.multiple_of(row0, MOE_BLOCK), MOE_BLOCK)], zsem)

        def start(e, c):
            @pl.when(pend_ref[e] > pstart_ref[e])
            def _():
                zero_block(pend_ref[e] - MOE_BLOCK).start()
            return c

        def wait(e, c):
            @pl.when(pend_ref[e] > pstart_ref[e])
            def _():
                zero_block(0).wait()
            return c

        def tail_start(j, c):
            zero_block(pend_ref[n_exp - 1] + j * MOE_BLOCK).start()
            return c

        def tail_wait(j, c):
            zero_block(0).wait()
            return c

        lax.fori_loop(0, n_exp, start, 0)
        lax.fori_loop(0, n_tail, tail_start, 0)
        lax.fori_loop(0, n_exp, wait, 0)
        lax.fori_loop(0, n_tail, tail_wait, 0)

    def start(t, c):
        for k in range(TOP_K):
            dst = pstart_ref[eidx_ref[k, t]] + pos_ref[k, t]
            _row_copy(h2p_ref, t, xs_ref, dst, sem).start(priority=k % N_DMA_QUEUES)
        return c

    def wait(t, c):
        for k in range(TOP_K):
            _row_copy(h2p_ref, t, xs_ref, 0, sem).wait()
        return c

    lax.fori_loop(0, tm, start, 0)
    lax.fori_loop(0, tm, wait, 0)


def _dispatch_call(pstart, pend, eidx, pos, h2p, n_rows):
    N, W = h2p.shape
    tm = min(TM_DISP, N)
    grid_spec = pltpu.PrefetchScalarGridSpec(
        num_scalar_prefetch=2,
        grid=(N // tm,),
        in_specs=[pl.BlockSpec((TOP_K, tm), lambda i, *_: (0, i), memory_space=pltpu.SMEM),
                  pl.BlockSpec((TOP_K, tm), lambda i, *_: (0, i), memory_space=pltpu.SMEM),
                  pl.BlockSpec((tm, W), lambda i, *_: (i, 0))],
        out_specs=pl.BlockSpec(memory_space=pl.ANY),
        scratch_shapes=[pltpu.VMEM((MOE_BLOCK, W), jnp.uint32),
                        pltpu.SemaphoreType.DMA(()),
                        pltpu.SemaphoreType.DMA(())],
    )
    return pl.pallas_call(
        _dispatch_kernel,
        grid_spec=grid_spec,
        out_shape=jax.ShapeDtypeStruct((n_rows, W), jnp.uint32),
        compiler_params=_cparams(1),
        name="dispatch",
    )(pstart, pend, eidx, pos, h2p)


def _experts_kernel(blk_e_ref, nused_ref, xs_ref, wg_ref, wu_ref, wd_ref, ys_ref, wgu_s, wd_s):
    i = pl.program_id(0)
    hd = wg_ref.shape[2]

    @pl.when((i == 0) | (blk_e_ref[i] != blk_e_ref[jnp.maximum(i - 1, 0)]))
    def _():
        wgu_s[:, 0:hd] = wg_ref[0].astype(BF16)
        wgu_s[:, hd:2 * hd] = wu_ref[0].astype(BF16)
        wd_s[...] = wd_ref[0].astype(BF16)

    @pl.when(i < nused_ref[0])
    def _():
        w = xs_ref[...]
        lo = pltpu.bitcast(w << 16, F32).astype(BF16)
        hi = pltpu.bitcast(w & jnp.uint32(0xFFFF0000), F32).astype(BF16)
        xb = jnp.concatenate([lo, hi], axis=1)
        h = _dot(xb, wgu_s[...])
        hid = (_silu(h[:, 0:hd]) * h[:, hd:2 * hd]).astype(BF16)
        ys_ref[...] = _dot(hid, wd_s[...])

    @pl.when(i >= nused_ref[0])
    def _():
        ys_ref[...] = jnp.zeros_like(ys_ref)


def _experts_call(blk_e, nused, xs, w_gate, w_up, w_down):
    n_rows, W = xs.shape
    n_blocks = n_rows // MOE_BLOCK
    E, D, Hd = w_gate.shape
    blk = lambda i, be, nu: (jnp.minimum(i, nu[0] - 1), 0)
    wmap = lambda i, be, nu: (be[i], 0, 0)
    grid_spec = pltpu.PrefetchScalarGridSpec(
        num_scalar_prefetch=2,
        grid=(n_blocks,),
        in_specs=[pl.BlockSpec((MOE_BLOCK, W), blk),
                  pl.BlockSpec((1, D, Hd), wmap),
                  pl.BlockSpec((1, D, Hd), wmap),
                  pl.BlockSpec((1, Hd, D), wmap)],
        out_specs=pl.BlockSpec((MOE_BLOCK, D), lambda i, be, nu: (i, 0)),
        scratch_shapes=[pltpu.VMEM((D, 2 * Hd), BF16), pltpu.VMEM((Hd, D), BF16)],
    )
    return pl.pallas_call(
        _experts_kernel,
        grid_spec=grid_spec,
        out_shape=jax.ShapeDtypeStruct((n_rows, D), F32),
        compiler_params=_cparams(1),
        name="experts",
    )(blk_e, nused, xs, w_gate, w_up, w_down)


def _combine_kernel(pstart_ref, eidx_ref, pos_ref, ys_ref, wgt_ref, h2_ref, x1_ref, ga_ref,
                    g2_ref, b2_ref, wsg_ref, wsu_ref, wsd_ref, o_ref, buf_ref, sem, *, alpha):
    tm = h2_ref.shape[1]

    def start(t, c):
        for k in range(TOP_K):
            src = pstart_ref[eidx_ref[k, t]] + pos_ref[k, t]
            _row_copy(ys_ref, src, buf_ref.at[k], t, sem).start(priority=k % N_DMA_QUEUES)
        return c

    def wait(t, c):
        for k in range(TOP_K):
            _row_copy(ys_ref, 0, buf_ref.at[k], t, sem).wait()
        return c

    lax.fori_loop(0, tm, start, 0)
    hb = h2_ref[0]
    hid = (_silu(_dot(hb, wsg_ref[...])) * _dot(hb, wsu_ref[...])).astype(BF16)
    y = _dot(hid, wsd_ref[...])
    lax.fori_loop(0, tm, wait, 0)
    wgt = wgt_ref[...]
    for k in range(TOP_K):
        y = y + buf_ref[k] * wgt[:, k:k + 1]
    o_ref[0] = _normalize_rows(alpha * x1_ref[0] + (1.0 + ga_ref[0]) * y) * g2_ref[...] + b2_ref[...]


def _combine_call(pstart, eidx, pos, ys, wgt_t, h2, x1, mod3, ln_g, ln_b, wsg, wsu, wsd, alpha):
    B, T, D = x1.shape
    tm = min(TM_COMB, T)
    nt = T // tm
    tokc = lambda b, i, *_: (0, b * nt + i)
    tokr = lambda b, i, *_: (b * nt + i, 0)
    row = lambda b, i, *_: (b, i, 0)
    const2 = lambda b, i, *_: (0, 0)
    grid_spec = pltpu.PrefetchScalarGridSpec(
        num_scalar_prefetch=1,
        grid=(B, nt),
        in_specs=[pl.BlockSpec((TOP_K, tm), tokc, memory_space=pltpu.SMEM),
                  pl.BlockSpec((TOP_K, tm), tokc, memory_space=pltpu.SMEM),
                  pl.BlockSpec(memory_space=pl.ANY),
                  pl.BlockSpec((tm, TOP_K), tokr),
                  pl.BlockSpec((1, tm, D), row),
                  pl.BlockSpec((1, tm, D), row),
                  pl.BlockSpec((1, 1, D), lambda b, i, *_: (b * 6 + 5, 0, 0)),
                  pl.BlockSpec((1, D), const2),
                  pl.BlockSpec((1, D), const2),
                  pl.BlockSpec(wsg.shape, const2),
                  pl.BlockSpec(wsu.shape, const2),
                  pl.BlockSpec(wsd.shape, const2)],
        out_specs=pl.BlockSpec((1, tm, D), row),
        scratch_shapes=[pltpu.VMEM((TOP_K, tm, D), F32),
                        pltpu.SemaphoreType.DMA(())],
    )
    return pl.pallas_call(
        functools.partial(_combine_kernel, alpha=alpha),
        grid_spec=grid_spec,
        out_shape=jax.ShapeDtypeStruct((B, T, D), F32),
        compiler_params=_cparams(2),
        name="combine",
    )(pstart, eidx, pos, ys, wgt_t, h2, x1, mod3, ln_g, ln_b, wsg, wsu, wsd)


def _pad_heads(w, n_heads, dh, side):
    D = w.shape[0]
    w = w.reshape(D, n_heads, dh)
    z = jnp.zeros_like(w)
    if side is None:
        out = jnp.concatenate([w, z], axis=-1)
    else:
        lo = jnp.concatenate([w, z], axis=-1)
        hi = jnp.concatenate([z, w], axis=-1)
        out = jnp.where(side[None, :, None], hi, lo)
    return out.reshape(D, n_heads * 2 * dh)


def _layout_w_in(w_in):
    D = w_in.shape[0]
    H, dh, G = NSA_HEADS, NSA_HEAD_DIM, NSA_KV_GROUPS
    o = 0
    wq = w_in[:, o:o + H * dh]; o += H * dh
    wcmp = w_in[:, o:o + 2 * G * dh]; o += 2 * G * dh
    wkvn = w_in[:, o:o + 4 * G * dh]; o += 4 * G * dh
    wng = w_in[:, o:o + H * 3]; o += H * 3
    wgq = w_in[:, o:o + GLA_HEADS * GLA_DK]; o += GLA_HEADS * GLA_DK
    wgk = w_in[:, o:o + GLA_HEADS * GLA_DK]; o += GLA_HEADS * GLA_DK
    wgv = w_in[:, o:o + GLA_HEADS * GLA_DV]; o += GLA_HEADS * GLA_DV
    wglr = w_in[:, o:o + GLA_GATE_RANK]; o += GLA_GATE_RANK
    wgr = w_in[:, o:o + GLA_HEADS * GLA_DV]; o += GLA_HEADS * GLA_DV
    high_half = (jnp.arange(H) // NSA_HPG) == 1
    wqn = _pad_heads(wq * (dh ** -0.5), H, dh, high_half)
    ng = wng.reshape(D, G, NSA_HPG * 3)
    wgate = jnp.concatenate([ng, jnp.zeros((D, G, LANES - NSA_HPG * 3), w_in.dtype)], -1).reshape(D, G * LANES)
    wgqk = jnp.concatenate([_pad_heads(wgq * (GLA_DK ** -0.5), GLA_HEADS, GLA_DK, None),
                            _pad_heads(wgk, GLA_HEADS, GLA_DK, None)], axis=1)
    wglr_p = jnp.concatenate([wglr, jnp.zeros((D, LANES - GLA_GATE_RANK), w_in.dtype)], axis=1)
    w_all = jnp.concatenate([wqn, wkvn, wcmp, wgate, wgqk, wgv, wglr_p, wgr], axis=1).astype(BF16)
    return w_all, wgv.T.astype(BF16)


def _overlap_matrix(n_cp, n_sp, seq_len):
    n_cmp = seq_len // CMP_STRIDE - CMP_BLOCK // CMP_STRIDE + 1
    n_slc = seq_len // SLC_BLOCK
    cs = np.arange(n_cmp) * CMP_STRIDE
    ce = cs + CMP_BLOCK - 1
    ss = np.arange(n_slc) * SLC_BLOCK
    se = ss + SLC_BLOCK - 1
    ov = np.clip(np.minimum(ce[:, None], se[None]) - np.maximum(cs[:, None], ss[None]) + 1, 0, None)
    out = np.zeros((n_cp, n_sp), np.float32)
    out[:n_cmp, :n_slc] = ov.astype(np.float32) / CMP_BLOCK
    return jnp.asarray(out, dtype=BF16)


def _round_up(a, m):
    return (a + m - 1) // m * m


def _token_mixer(x, mod3, w_in, cmp_pos_k, cmp_w1_k, cmp_w2_k, cmp_pos_v, cmp_w1_v, cmp_w2_v,
                 gla_w_a2, gla_b_a2, gla_norm_w):
    B, T, D = x.shape
    G, dh = NSA_KV_GROUPS, NSA_HEAD_DIM
    w_all, w_vt = _layout_w_in(w_in)
    qn, kvn, cmpkv, gates, gqk, gv, glr, gr, gvt = _inproj_call(x, mod3, w_all, w_vt)

    n_sub = T // CMP_STRIDE
    pos2 = jnp.tile(jnp.stack([cmp_pos_k, cmp_pos_v]), (1, 1, G))
    w1 = jnp.stack([cmp_w1_k, cmp_w1_v])
    w2 = jnp.stack([cmp_w2_k, cmp_w2_v])
    w2p = jnp.stack([jnp.pad(w2, ((0, 0), (0, 0), (g * dh, (G - 1 - g) * dh))) for g in range(G)], axis=1)
    kcvc = _compress_call(cmpkv, pos2, w1, w2p)
    n_cp = _round_up(n_sub, LANES)
    kcvc = jnp.pad(kcvc, ((0, 0), (0, 0), (0, n_cp - n_sub), (0, 0)))
    n_slc = T // SLC_BLOCK
    n_sp = _round_up(n_slc, LANES)
    cmp_last = np.arange(n_cp) * CMP_STRIDE + (CMP_BLOCK - 1)
    cfeat = _position_features(cmp_last, n_slc)
    cfeat[:, :n_slc] = 0.0
    kcx = jnp.concatenate([kcvc[0], jnp.broadcast_to(jnp.asarray(cfeat, BF16), (B, n_cp, LANES))], axis=2)
    kfeat = jnp.asarray(_position_features(np.arange(T), n_slc), BF16)
    o_nsa = _nsa_call(qn, gates, kcx, kcvc[1], kvn, kfeat, _overlap_matrix(n_cp, n_sp, T))

    wa = gla_w_a2.reshape(GLA_GATE_RANK, GLA_HEADS, GLA_DK)
    wa_pad = jnp.zeros((LANES, GLA_HEADS, LANES), F32).at[:GLA_GATE_RANK, :, :GLA_DK].set(wa)
    wa_pad = wa_pad.reshape(LANES, GLA_HEADS * LANES)
    ba_pad = jnp.zeros((GLA_HEADS, LANES), F32).at[:, :GLA_DK].set(gla_b_a2.reshape(GLA_HEADS, GLA_DK))
    ba_pad = ba_pad.reshape(1, GLA_HEADS * LANES)
    o_gla = _gla_call(gqk, gv, gvt, glr, gr, wa_pad, ba_pad, gla_norm_w.reshape(1, GLA_DV))
    return o_nsa, o_gla


def _layer(x, c_pad, l, alpha, w_mod, b_mod, w_in, cmp_pos_k, cmp_w1_k, cmp_w2_k, cmp_pos_v, cmp_w1_v,
           cmp_w2_v, gla_w_a2, gla_b_a2, gla_norm_w, w_o, ln1_g, ln1_b, w_router, router_bias,
           w_e_gate, w_e_up, w_e_down, w_s_gate, w_s_up, w_s_down, ln2_g, ln2_b):
    B, T, D = x.shape
    N = B * T
    E = N_EXPERTS
    mod = _mod_call(c_pad, w_mod[l], b_mod[l].reshape(1, -1))
    mod3 = mod[:B].reshape(B * 6, 1, D)

    o_nsa, o_gla = _token_mixer(x, mod3, w_in[l], cmp_pos_k[l], cmp_w1_k[l], cmp_w2_k[l], cmp_pos_v[l],
                                cmp_w1_v[l], cmp_w2_v[l], gla_w_a2[l], gla_b_a2[l], gla_norm_w[l])

    wo = w_o[l]
    n_nsa = NSA_HEADS * NSA_HEAD_DIM
    high_half = (jnp.arange(NSA_HEADS) // NSA_HPG) == 1
    wo_nsa = _pad_heads(wo[:n_nsa].T, NSA_HEADS, NSA_HEAD_DIM, high_half).T
    wo_pad = jnp.concatenate([wo_nsa, wo[n_nsa:]], axis=0).astype(BF16)
    wr_t = w_router[l].T
    wr_hi = wr_t.astype(BF16)
    wr_lo = (wr_t - wr_hi.astype(F32)).astype(BF16)
    tm_o = min(TM_OPROJ, T)
    tri = jnp.asarray(np.triu(np.ones((tm_o, tm_o), np.float32), 1), dtype=BF16)
    x1, h2, h2p, eidx, wgt, pos, counts = _oproj_call(
        o_nsa, o_gla, x, mod3, ln1_g[l].reshape(1, D), ln1_b[l].reshape(1, D), wo_pad, wr_hi, wr_lo,
        router_bias[l].reshape(E, 1), tri, alpha)

    cnt = counts[:, 0].astype(jnp.int32)
    padded = (cnt + MOE_BLOCK - 1) // MOE_BLOCK * MOE_BLOCK
    pend = jnp.cumsum(padded)
    pstart = pend - padded
    n_blocks = -(-N * TOP_K // MOE_BLOCK) + E
    blk_start = jnp.arange(n_blocks, dtype=jnp.int32) * MOE_BLOCK
    nused = (pend[-1] // MOE_BLOCK).astype(jnp.int32)
    first_row = jnp.minimum(blk_start, pend[-1] - 1)
    blk_e = jnp.sum((pend[None, :] <= first_row[:, None]).astype(jnp.int32), axis=1)
    blk_e = jnp.minimum(blk_e, E - 1)

    xs = _dispatch_call(pstart, pend, eidx, pos, h2p.reshape(N, D // 2), n_blocks * MOE_BLOCK)
    ys = _experts_call(blk_e, nused.reshape(1), xs, w_e_gate[l], w_e_up[l], w_e_down[l])
    return _combine_call(pstart, eidx, pos, ys, wgt.T, h2, x1, mod3, ln2_g[l].reshape(1, D),
                         ln2_b[l].reshape(1, D), w_s_gate[l].astype(BF16), w_s_up[l].astype(BF16),
                         w_s_down[l].astype(BF16), alpha)


def kernel(x, c, w_mod, b_mod, w_in, cmp_pos_k, cmp_w1_k, cmp_w2_k, cmp_pos_v, cmp_w1_v, cmp_w2_v, gla_w_a2, gla_b_a2, gla_norm_w, w_o, ln1_g, ln1_b, w_router, router_bias, w_e_gate, w_e_up, w_e_down, w_s_gate, w_s_up, w_s_down, ln2_g, ln2_b):
    depth = w_mod.shape[0]
    alpha = (2.0 * depth) ** 0.25
    B = x.shape[0]
    c_pad = jnp.pad(c, ((0, (-B) % 8), (0, 0)))
    for l in range(depth):
        x = _layer(x, c_pad, l, alpha, w_mod, b_mod, w_in, cmp_pos_k, cmp_w1_k, cmp_w2_k, cmp_pos_v,
                   cmp_w1_v, cmp_w2_v, gla_w_a2, gla_b_a2, gla_norm_w, w_o, ln1_g, ln1_b, w_router,
                   router_bias, w_e_gate, w_e_up, w_e_down, w_s_gate, w_s_up, w_s_down, ln2_g, ln2_b)
    return x
```

```python
import functools

import jax
import jax.numpy as jnp
import numpy as np
from jax import lax
from jax.experimental import pallas as pl
from jax.experimental.pallas import tpu as pltpu

NSA_HEADS = 8
NSA_KV_GROUPS = 2
NSA_HPG = NSA_HEADS // NSA_KV_GROUPS
NSA_HEAD_DIM = 64
CMP_BLOCK = 32
CMP_STRIDE = 16
CMP_HIDDEN = 256
SLC_BLOCK = 64
SLC_TOPN = 16
WINDOW = 512
FORCE_BONUS = 1e4
GLA_HEADS = 4
GLA_DK = 64
GLA_DV = 128
GLA_GATE_RANK = 16
GLA_TAU = 16.0
N_EXPERTS = 256
TOP_K = 8
N_EXPERT_GROUPS = 8
TOPK_GROUPS = 4
EXPERT_HIDDEN = 256
ROUTE_SCALE = 2.5
LN_EPS = 1e-5
NEG = -1e30
TAKEN = -3e38

LANES = 128
VMEM_LIMIT = 56 * 1024 * 1024
N_DMA_QUEUES = 2

TM_PROJ = 256
TQ_NSA = 256
TK_SLC = 512
GLA_CHUNK = 128
GLA_SUB = 16
GLA_EXP_CLAMP = 80.0
TM_OPROJ = 256
MOE_BLOCK = 128
TM_DISP = 128
TM_COMB = 128

BF16 = jnp.bfloat16
F32 = jnp.float32


def _cparams(n_axes):
    return pltpu.CompilerParams(
        dimension_semantics=("arbitrary",) * n_axes, vmem_limit_bytes=VMEM_LIMIT)


def _dot(a, b):
    return jnp.dot(a, b, preferred_element_type=F32)


def _dot_nt(a, b):
    return lax.dot_general(a, b, (((1,), (1,)), ((), ())), preferred_element_type=F32)


def _split(a):
    hi = a.astype(BF16)
    lo = (a - hi.astype(F32)).astype(BF16)
    return hi, lo


def _dot3(a, b):
    ah, al = _split(a)
    bh, bl = _split(b)
    return _dot(ah, bh) + _dot(ah, bl) + _dot(al, bh)


def _silu(x):
    return x * (1.0 / (1.0 + jnp.exp(-x)))


def _sigmoid(x):
    return 1.0 / (1.0 + jnp.exp(-x))


def _normalize_rows(x):
    mu = jnp.mean(x, axis=-1, keepdims=True)
    xc = x - mu
    var = jnp.mean(xc * xc, axis=-1, keepdims=True)
    return xc * lax.rsqrt(var + LN_EPS)


def _mod_kernel(c_ref, w_ref, b_ref, o_ref):
    o_ref[...] = _dot3(_silu(c_ref[...]), w_ref[...]) + b_ref[...]


def _mod_call(c_pad, w_mod, b_mod):
    rows, d = c_pad.shape
    n_out = w_mod.shape[1]
    return pl.pallas_call(
        _mod_kernel,
        grid=(n_out // d,),
        in_specs=[pl.BlockSpec((rows, d), lambda j: (0, 0)),
                  pl.BlockSpec((d, d), lambda j: (0, j)),
                  pl.BlockSpec((1, d), lambda j: (0, j))],
        out_specs=pl.BlockSpec((rows, d), lambda j: (0, j)),
        out_shape=jax.ShapeDtypeStruct((rows, n_out), F32),
        compiler_params=_cparams(1),
        name="mod",
    )(c_pad, w_mod, b_mod)


_PROJ_COLS = (("qn", 1024, BF16), ("kvn", 512, BF16), ("cmp", 256, F32), ("gate", 256, F32),
              ("gqk", 1024, F32), ("gv", 512, BF16), ("glr", 128, F32), ("gr", 512, F32))


def _inproj_kernel(x_ref, sh_ref, sc_ref, w_ref, wvt_ref,
                   qn_ref, kvn_ref, cmp_ref, gate_ref, gqk_ref, gv_ref, glr_ref, gr_ref, gvt_ref):
    h = _normalize_rows(x_ref[0]) * (1.0 + sc_ref[0]) + sh_ref[0]
    hb = h.astype(BF16)
    outs = (qn_ref, kvn_ref, cmp_ref, gate_ref, gqk_ref, gv_ref, glr_ref, gr_ref)
    c0 = 0
    for (name, width, dt), o_ref in zip(_PROJ_COLS, outs):
        p = _dot(hb, w_ref[:, c0:c0 + width])
        if name == "gate":
            p = _sigmoid(p)
        o_ref[0] = p.astype(dt)
        c0 += width
    gvt_ref[0] = _dot_nt(wvt_ref[...], hb).astype(BF16)


def _inproj_call(x, mod3, w_all, w_vt):
    B, T, D = x.shape
    tm = min(TM_PROJ, T)
    ctot = w_all.shape[1]
    out_shape = [jax.ShapeDtypeStruct((B, T, wdt), dt) for _, wdt, dt in _PROJ_COLS]
    out_shape.append(jax.ShapeDtypeStruct((B, w_vt.shape[0], T), BF16))
    out_specs = [pl.BlockSpec((1, tm, wdt), lambda b, i: (b, i, 0)) for _, wdt, _ in _PROJ_COLS]
    out_specs.append(pl.BlockSpec((1, w_vt.shape[0], tm), lambda b, i: (b, 0, i)))
    return pl.pallas_call(
        _inproj_kernel,
        grid=(B, T // tm),
        in_specs=[pl.BlockSpec((1, tm, D), lambda b, i: (b, i, 0)),
                  pl.BlockSpec((1, 1, D), lambda b, i: (b * 6 + 0, 0, 0)),
                  pl.BlockSpec((1, 1, D), lambda b, i: (b * 6 + 1, 0, 0)),
                  pl.BlockSpec((D, ctot), lambda b, i: (0, 0)),
                  pl.BlockSpec(w_vt.shape, lambda b, i: (0, 0))],
        out_specs=out_specs,
        out_shape=out_shape,
        compiler_params=_cparams(2),
        name="inproj",
    )(x, mod3, mod3, w_all, w_vt)


def _compress_kernel(x_ref, pos_ref, w1_ref, w2_ref, o_ref):
    G, dh, st = NSA_KV_GROUPS, NSA_HEAD_DIM, CMP_STRIDE
    n_sub = o_ref.shape[2]
    zero_w = jnp.zeros((dh, w1_ref.shape[2]), BF16)
    out = jnp.zeros((n_sub, G * dh), F32)
    for g in range(G):
        halves = [jnp.zeros((n_sub, w1_ref.shape[2]), F32) for _ in range(CMP_BLOCK // st)]
        for l in range(st):
            x_l = x_ref[0, pl.ds(l, n_sub, stride=st), :]
            for half in range(len(halves)):
                r = half * st + l
                a = (x_l + pos_ref[0, r:r + 1, :]).astype(BF16)
                w = w1_ref[0, r * dh:(r + 1) * dh, :].astype(BF16)
                w = jnp.concatenate([w, zero_w] if g == 0 else [zero_w, w], axis=0)
                halves[half] = halves[half] + _dot(a, w)
        hid = halves[0] + pltpu.roll(halves[1], n_sub - 1, 0)
        out = out + _dot(_silu(hid).astype(BF16), w2_ref[0, g].astype(BF16))
    o_ref[0, 0] = out.astype(o_ref.dtype)


def _compress_call(cmpkv, pos2, w1, w2p):
    B, T, _ = cmpkv.shape
    n_sub = T // CMP_STRIDE
    return pl.pallas_call(
        _compress_kernel,
        grid=(2, B),
        in_specs=[pl.BlockSpec((1, T, LANES), lambda s, b: (b, 0, s)),
                  pl.BlockSpec((1,) + pos2.shape[1:], lambda s, b: (s, 0, 0)),
                  pl.BlockSpec((1,) + w1.shape[1:], lambda s, b: (s, 0, 0)),
                  pl.BlockSpec((1,) + w2p.shape[1:], lambda s, b: (s, 0, 0, 0))],
        out_specs=pl.BlockSpec((1, 1, n_sub, LANES), lambda s, b: (s, b, 0, 0)),
        out_shape=jax.ShapeDtypeStruct((2, B, n_sub, LANES), BF16),
        compiler_params=_cparams(2),
        name="compress",
    )(cmpkv, pos2, w1, w2p)


def _position_features(pos, n_slc):
    assert n_slc + 2 <= LANES and pos.max() // SLC_BLOCK < 256
    f = np.zeros((pos.shape[0], LANES), np.float32)
    blk = pos // SLC_BLOCK
    onehot = blk < n_slc
    f[np.arange(pos.shape[0])[onehot], blk[onehot]] = 1.0
    f[:, n_slc] = blk
    f[:, n_slc + 1] = pos % SLC_BLOCK
    return f


def _nsa_kernel(q_ref, gate_ref, kc_ref, vc_ref, kv_ref, kf_ref, ov_ref, o_ref,
                st_ref, kx_ref, kwx_ref, *, seq_len):
    g = pl.program_id(1)
    qi = pl.program_id(2)
    tq = q_ref.shape[1]
    P = NSA_HPG
    R = P * tq
    n_cp = kc_ref.shape[1]
    n_sp = ov_ref.shape[1]
    n_slc = seq_len // SLC_BLOCK
    n_sel = min(SLC_TOPN, n_slc)
    q0 = qi * tq

    @pl.when((g == 0) & (qi == 0))
    def _():
        kx_ref[:, 0:LANES] = kv_ref[0, :, 0:LANES]
        kx_ref[:, LANES:2 * LANES] = kf_ref[...]
        kwx_ref[:, 0:LANES] = kv_ref[0, :, 2 * LANES:3 * LANES]
        kwx_ref[:, LANES:2 * LANES] = kf_ref[...]

    qblk = q_ref[0]
    gscale = jnp.where(g == 0, 1.0, 2.0 ** (-P)).astype(F32)
    lane = lax.broadcasted_iota(jnp.int32, (1, LANES), 1)
    alibi_lanes = jnp.where(lane == n_slc, float(SLC_BLOCK), jnp.where(lane == n_slc + 1, 1.0, 0.0))
    q_heads = [qblk[:, p * LANES:(p + 1) * LANES] for p in range(P)]
    f_pos = [jnp.broadcast_to(alibi_lanes * (2.0 ** (-(p + 1)) * gscale), (tq, LANES)) for p in range(P)]
    Qa = jnp.concatenate([jnp.concatenate([q_heads[p], f_pos[p].astype(BF16)], axis=1)
                          for p in range(P)], axis=0)
    t_q = q0 + lax.broadcasted_iota(jnp.int32, (tq, 1), 0)
    t_row = jnp.concatenate([t_q] * P, axis=0)

    def heads(x):
        return jnp.concatenate([x] * P, axis=0)

    n_idx = lax.broadcasted_iota(jnp.int32, (1, n_cp), 1)
    ce = n_idx * CMP_STRIDE + (CMP_BLOCK - 1)
    s_c = _dot_nt(Qa, kc_ref[0]) + heads(jnp.where(ce <= t_q, 0.0, NEG))
    e_c = jnp.exp(s_c - jnp.max(s_c, axis=-1, keepdims=True))
    inv_c = jnp.where(t_row >= CMP_BLOCK - 1,
                      1.0 / jnp.maximum(jnp.sum(e_c, axis=-1, keepdims=True), 1e-30), 0.0)
    p_c = e_c * inv_c
    o_cmp = _dot(p_c.astype(BF16), vc_ref[0])

    p_sum = p_c[0:tq]
    for p in range(1, P):
        p_sum = p_sum + p_c[p * tq:(p + 1) * tq]
    ph, plo = _split(p_sum)
    imp = _dot(ph, ov_ref[...]) + _dot(plo, ov_ref[...])
    j_idx = lax.broadcasted_iota(jnp.int32, (1, n_sp), 1)
    t_blk = t_q // SLC_BLOCK
    causal_blk = j_idx <= t_blk
    forced = (j_idx == 0) | (j_idx == t_blk) | (j_idx == t_blk - 1)
    score = jnp.where(causal_blk, imp + jnp.where(forced, FORCE_BONUS, 0.0), NEG)
    st_ref[...] = score.T
    n_rk = _round_up(n_slc, 8)
    s_all = st_ref[0:n_rk, :]
    jrow = lax.broadcasted_iota(jnp.int32, (n_rk, tq), 0)

    def rank_body(jp, cnt):
        row = st_ref[pl.ds(jp, 1), :]
        ge = jnp.where(row >= s_all, 1.0, 0.0)
        gt = jnp.where(row > s_all, 1.0, 0.0)
        return cnt + jnp.where(jrow > jp, ge, gt)

    n_causal = (q0 + tq) // SLC_BLOCK
    cnt = lax.fori_loop(0, n_causal, rank_body, jnp.zeros((n_rk, tq), F32))
    unsel = jnp.where(cnt < n_sel, 0.0, NEG)
    if n_rk < n_sp:
        unsel = jnp.concatenate([unsel, jnp.zeros((n_sp - n_rk, tq), F32)], axis=0)
    sel_bias = unsel.T

    tk = min(TK_SLC, seq_len)
    Qs = jnp.concatenate([jnp.concatenate([q_heads[p], (sel_bias + f_pos[p]).astype(BF16)], axis=1)
                          for p in range(P)], axis=0)

    def slc_tile(k0, carry, bias):
        m_i, l_i, acc = carry
        s = _dot_nt(Qs, kx_ref[pl.ds(k0, tk), :])
        if bias is not None:
            s = s + bias
        m_new = jnp.maximum(m_i, jnp.max(s, axis=-1, keepdims=True))
        e = jnp.exp(s - m_new)
        alpha = jnp.exp(m_i - m_new)
        l_new = alpha * l_i + jnp.sum(e, axis=-1, keepdims=True)
        acc_new = alpha * acc + _dot(e.astype(BF16), kv_ref[0, pl.ds(k0, tk), LANES:2 * LANES])
        return m_new, l_new, acc_new

    n_full = q0 // tk
    init = (jnp.full((R, 1), NEG, F32), jnp.zeros((R, 1), F32), jnp.zeros((R, LANES), F32))
    carry = lax.fori_loop(0, n_full, lambda it, c: slc_tile(pl.multiple_of(it * tk, tk), c, None), init)
    k_last = pl.multiple_of(n_full * tk, tk)
    kpos = k_last + lax.broadcasted_iota(jnp.int32, (1, tk), 1)
    _, l_s, acc_s = slc_tile(k_last, carry, heads(jnp.where(kpos <= t_q, 0.0, NEG)))
    o_slc = acc_s / jnp.maximum(l_s, 1e-30)

    wlen = min(WINDOW + tq, seq_len)
    w0 = pl.multiple_of(jnp.minimum(jnp.maximum(q0 - WINDOW, 0), seq_len - wlen), tq)
    wpos = w0 + lax.broadcasted_iota(jnp.int32, (1, wlen), 1)
    win_bias = jnp.where(wpos <= t_q, jnp.where(wpos > t_q - WINDOW, 0.0, NEG), NEG)
    s_w = _dot_nt(Qa, kwx_ref[pl.ds(w0, wlen), :]) + heads(win_bias)
    e_w = jnp.exp(s_w - jnp.max(s_w, axis=-1, keepdims=True))
    o_win = _dot(e_w.astype(BF16), kv_ref[0, pl.ds(w0, wlen), 3 * LANES:4 * LANES])
    o_win = o_win / jnp.maximum(jnp.sum(e_w, axis=-1, keepdims=True), 1e-30)

    gates = gate_ref[0]
    lane = lax.broadcasted_iota(jnp.int32, (1, LANES), 1)
    live = (lane // NSA_HEAD_DIM) == g
    pieces = []
    for p in range(P):
        rows = slice(p * tq, (p + 1) * tq)
        o_p = (gates[:, 3 * p + 0:3 * p + 1] * o_cmp[rows]
               + gates[:, 3 * p + 1:3 * p + 2] * o_slc[rows]
               + gates[:, 3 * p + 2:3 * p + 3] * o_win[rows])
        pieces.append(jnp.where(live, o_p, 0.0))
    o_ref[0] = jnp.concatenate(pieces, axis=1).astype(o_ref.dtype)


def _nsa_call(qn, gates, kcx, vc, kvn, kfeat, overlap):
    B, T, _ = qn.shape
    G, P = NSA_KV_GROUPS, NSA_HPG
    tq = min(TQ_NSA, T)
    n_cp = kcx.shape[1]
    n_sp = overlap.shape[1]
    assert (T // SLC_BLOCK) % 8 == 0 and min(TK_SLC, T) % tq == 0
    return pl.pallas_call(
        functools.partial(_nsa_kernel, seq_len=T),
        grid=(B, G, T // tq),
        in_specs=[pl.BlockSpec((1, tq, P * LANES), lambda b, g, i: (b, i, g)),
                  pl.BlockSpec((1, tq, LANES), lambda b, g, i: (b, i, g)),
                  pl.BlockSpec((1, n_cp, 2 * LANES), lambda b, g, i: (b, 0, 0)),
                  pl.BlockSpec((1, n_cp, LANES), lambda b, g, i: (b, 0, 0)),
                  pl.BlockSpec((1, T, 4 * LANES), lambda b, g, i: (b, 0, 0)),
                  pl.BlockSpec((T, LANES), lambda b, g, i: (0, 0)),
                  pl.BlockSpec((n_cp, n_sp), lambda b, g, i: (0, 0))],
        out_specs=pl.BlockSpec((1, tq, P * LANES), lambda b, g, i: (b, i, g)),
        out_shape=jax.ShapeDtypeStruct((B, T, G * P * LANES), BF16),
        scratch_shapes=[pltpu.VMEM((n_sp, tq), F32),
                        pltpu.VMEM((T, 2 * LANES), BF16),
                        pltpu.VMEM((T, 2 * LANES), BF16)],
        compiler_params=_cparams(3),
        name="nsa",
    )(qn, gates, kcx, vc, kvn, kfeat, overlap)


def _gla_kernel(qk_ref, v_ref, vt_ref, glr_ref, r_ref, wa_ref, ba_ref, nw_ref, o_ref, st_ref):
    ci = pl.program_id(1)
    C = qk_ref.shape[1]
    H = GLA_HEADS
    n_sub = C // GLA_SUB

    @pl.when(ci == 0)
    def _():
        st_ref[...] = jnp.zeros_like(st_ref)

    z = _dot3(glr_ref[0], wa_ref[...]) + ba_ref[...]
    gdec = -(jnp.maximum(-z, 0.0) + jnp.log(1.0 + jnp.exp(-jnp.abs(z)))) * (1.0 / GLA_TAU)
    rowi = lax.broadcasted_iota(jnp.int32, (C, 1), 0)
    b = gdec
    sh = 1
    while sh < C:
        b = b + jnp.where(rowi >= sh, pltpu.roll(b, sh, 0), 0.0)
        sh *= 2

    qk = qk_ref[0]
    r_i = lax.broadcasted_iota(jnp.int32, (C, C), 0)
    c_i = lax.broadcasted_iota(jnp.int32, (C, C), 1)
    causal = c_i <= r_i
    for h in range(H):
        cols = slice(h * LANES, (h + 1) * LANES)
        qh = qk[:, h * LANES:(h + 1) * LANES]
        kh = qk[:, (H + h) * LANES:(H + h + 1) * LANES]
        bh = b[:, cols]
        bl = bh[C - 1:C, :]
        st = st_ref[h]
        o_inter = _dot_nt((qh * jnp.exp(bh)).astype(BF16), st.astype(BF16))
        a_rows = []
        for i in range(n_sub):
            rs = slice(i * GLA_SUB, (i + 1) * GLA_SUB)
            ref_b = bh[i * GLA_SUB:i * GLA_SUB + 1, :]
            q_i = (qh[rs] * jnp.exp(bh[rs] - ref_b)).astype(BF16)
            k_i = (kh * jnp.exp(jnp.minimum(ref_b - bh, GLA_EXP_CLAMP))).astype(BF16)
            a_rows.append(_dot_nt(q_i, k_i))
        attn = jnp.where(causal, jnp.concatenate(a_rows, axis=0), 0.0)
        o_h = o_inter + _dot(attn.astype(BF16), v_ref[0, :, cols])
        kd = (kh * jnp.exp(bl - bh)).astype(BF16)
        st_ref[h] = st * jnp.exp(bl) + _dot(vt_ref[0, h * GLA_DV:(h + 1) * GLA_DV, :], kd)
        o_n = _normalize_rows(o_h) * nw_ref[...]
        o_ref[0, :, cols] = (o_n * _silu(r_ref[0, :, cols])).astype(o_ref.dtype)


def _gla_call(gqk, gv, gvt, glr, gr, wa_pad, ba_pad, norm_w):
    B, T, _ = gqk.shape
    H = GLA_HEADS
    C = min(GLA_CHUNK, T)
    return pl.pallas_call(
        _gla_kernel,
        grid=(B, T // C),
        in_specs=[pl.BlockSpec((1, C, 2 * H * LANES), lambda b, i: (b, i, 0)),
                  pl.BlockSpec((1, C, H * GLA_DV), lambda b, i: (b, i, 0)),
                  pl.BlockSpec((1, H * GLA_DV, C), lambda b, i: (b, 0, i)),
                  pl.BlockSpec((1, C, LANES), lambda b, i: (b, i, 0)),
                  pl.BlockSpec((1, C, H * GLA_DV), lambda b, i: (b, i, 0)),
                  pl.BlockSpec(wa_pad.shape, lambda b, i: (0, 0)),
                  pl.BlockSpec(ba_pad.shape, lambda b, i: (0, 0)),
                  pl.BlockSpec(norm_w.shape, lambda b, i: (0, 0))],
        out_specs=pl.BlockSpec((1, C, H * GLA_DV), lambda b, i: (b, i, 0)),
        out_shape=jax.ShapeDtypeStruct((B, T, H * GLA_DV), BF16),
        scratch_shapes=[pltpu.VMEM((H, GLA_DV, LANES), F32)],
        compiler_params=_cparams(2),
        name="gla",
    )(gqk, gv, gvt, glr, gr, wa_pad, ba_pad, norm_w)


def _oproj_kernel(on_ref, og_ref, x_ref, ga_ref, sh_ref, sc_ref, g1_ref, b1_ref, wo_ref,
                  wrh_ref, wrl_ref, rb_ref, tri_ref,
                  x1_ref, h2_ref, eidx_ref, wgt_ref, pos_ref, cnt_ref, run_ref, *, alpha):
    first = (pl.program_id(0) == 0) & (pl.program_id(1) == 0)

    @pl.when(first)
    def _():
        run_ref[...] = jnp.zeros_like(run_ref)

    kn = on_ref.shape[2]
    y = _dot(on_ref[0], wo_ref[0:kn, :]) + _dot(og_ref[0], wo_ref[kn:, :])
    x1 = _normalize_rows(alpha * x_ref[0] + (1.0 + ga_ref[0]) * y) * g1_ref[...] + b1_ref[...]
    x1_ref[0] = x1
    h2 = _normalize_rows(x1) * (1.0 + sc_ref[0]) + sh_ref[0]
    h2_ref[0] = h2
    hh, hl = _split(h2)

    logit = _dot_nt(wrh_ref[...], hh) + _dot_nt(wrh_ref[...], hl) + _dot_nt(wrl_ref[...], hh)
    s = _sigmoid(logit)
    sb = s + rb_ref[...]
    E, tm = s.shape
    gsz = E // N_EXPERT_GROUPS
    gi = lax.broadcasted_iota(jnp.int32, (gsz, tm), 0).astype(F32)
    gscore = []
    for gidx in range(N_EXPERT_GROUPS):
        blk = sb[gidx * gsz:(gidx + 1) * gsz]
        m1 = jnp.max(blk, axis=0, keepdims=True)
        i1 = jnp.min(jnp.where(blk == m1, gi, float(gsz)), axis=0, keepdims=True)
        m2 = jnp.max(jnp.where(gi == i1, TAKEN, blk), axis=0, keepdims=True)
        gscore.append(m1 + m2)
    masked = []
    for gidx in range(N_EXPERT_GROUPS):
        rank = jnp.zeros((1, tm), F32)
        for other in range(N_EXPERT_GROUPS):
            if other == gidx:
                continue
            beats = (gscore[other] >= gscore[gidx]) if other < gidx else (gscore[other] > gscore[gidx])
            rank = rank + jnp.where(beats, 1.0, 0.0)
        masked.append(jnp.where(rank < TOPK_GROUPS, sb[gidx * gsz:(gidx + 1) * gsz], NEG))
    ms = jnp.concatenate(masked, axis=0)
    ei = lax.broadcasted_iota(jnp.int32, (E, tm), 0).astype(F32)
    picks, idxs, vals = [], [], []
    for _ in range(TOP_K):
        m = jnp.max(ms, axis=0, keepdims=True)
        ix = jnp.min(jnp.where(ms == m, ei, float(E)), axis=0, keepdims=True)
        pick = ei == ix
        ms = jnp.where(pick, TAKEN, ms)
        picks.append(pick)
        idxs.append(ix)
        vals.append(jnp.sum(jnp.where(pick, s, 0.0), axis=0, keepdims=True))
    wsum = vals[0]
    for v in vals[1:]:
        wsum = wsum + v
    sel = jnp.zeros((E, tm), F32)
    for pick in picks:
        sel = sel + jnp.where(pick, 1.0, 0.0)
    pos_all = run_ref[:, 0:1] + _dot(sel.astype(BF16), tri_ref[...])
    run_ref[...] = run_ref[...] + jnp.sum(sel, axis=1, keepdims=True)
    cnt_ref[...] = run_ref[...]
    for k in range(TOP_K):
        eidx_ref[k:k + 1, :] = idxs[k].astype(jnp.int32)
        wgt_ref[k:k + 1, :] = vals[k] / wsum * ROUTE_SCALE
        pos_ref[k:k + 1, :] = jnp.sum(jnp.where(picks[k], pos_all, 0.0), axis=0,
                                      keepdims=True).astype(jnp.int32)


def _oproj_call(o_nsa, o_gla, x, mod3, ln_g, ln_b, wo_pad, wr_hi, wr_lo, rbias, tri, alpha):
    B, T, D = x.shape
    tm = min(TM_OPROJ, T)
    N = B * T
    E = wr_hi.shape[0]
    nt = T // tm
    tok = lambda b, i: (0, b * nt + i)
    const2 = lambda b, i: (0, 0)
    return pl.pallas_call(
        functools.partial(_oproj_kernel, alpha=alpha),
        grid=(B, nt),
        in_specs=[pl.BlockSpec((1, tm, o_nsa.shape[2]), lambda b, i: (b, i, 0)),
                  pl.BlockSpec((1, tm, o_gla.shape[2]), lambda b, i: (b, i, 0)),
                  pl.BlockSpec((1, tm, D), lambda b, i: (b, i, 0)),
                  pl.BlockSpec((1, 1, D), lambda b, i: (b * 6 + 2, 0, 0)),
                  pl.BlockSpec((1, 1, D), lambda b, i: (b * 6 + 3, 0, 0)),
                  pl.BlockSpec((1, 1, D), lambda b, i: (b * 6 + 4, 0, 0)),
                  pl.BlockSpec((1, D), const2),
                  pl.BlockSpec((1, D), const2),
                  pl.BlockSpec(wo_pad.shape, const2),
                  pl.BlockSpec(wr_hi.shape, const2),
                  pl.BlockSpec(wr_lo.shape, const2),
                  pl.BlockSpec(rbias.shape, const2),
                  pl.BlockSpec(tri.shape, const2)],
        out_specs=[pl.BlockSpec((1, tm, D), lambda b, i: (b, i, 0)),
                   pl.BlockSpec((1, tm, D), lambda b, i: (b, i, 0)),
                   pl.BlockSpec((TOP_K, tm), tok),
                   pl.BlockSpec((TOP_K, tm), tok),
                   pl.BlockSpec((TOP_K, tm), tok),
                   pl.BlockSpec((E, LANES), const2)],
        out_shape=[jax.ShapeDtypeStruct((B, T, D), F32),
                   jax.ShapeDtypeStruct((B, T, D), F32),
                   jax.ShapeDtypeStruct((TOP_K, N), jnp.int32),
                   jax.ShapeDtypeStruct((TOP_K, N), F32),
                   jax.ShapeDtypeStruct((TOP_K, N), jnp.int32),
                   jax.ShapeDtypeStruct((E, LANES), F32)],
        scratch_shapes=[pltpu.VMEM((E, LANES), F32)],
        compiler_params=_cparams(2),
        name="oproj",
    )(o_nsa, o_gla, x, mod3, mod3, mod3, ln_g, ln_b, wo_pad, wr_hi, wr_lo, rbias, tri)


def _row_copy(src_ref, src_row, dst_ref, dst_row, sem):
    return pltpu.make_async_copy(src_ref.at[pl.ds(src_row, 1)], dst_ref.at[pl.ds(dst_row, 1)], sem)


def _dispatch_kernel(pstart_ref, pend_ref, eidx_ref, pos_ref, h2p_ref, xs_ref, zero_ref, sem, zsem):
    step = pl.program_id(0)
    tm = h2p_ref.shape[0]
    n_exp = pstart_ref.shape[0]

    @pl.when(step == 0)
    def _():
        zero_ref[...] = jnp.zeros_like(zero_ref)
        n_tail = xs_ref.shape[0] // MOE_BLOCK - pend_ref[n_exp - 1] // MOE_BLOCK

        def zero_block(row0):
            return pltpu.make_async_copy(
                zero_ref, xs_ref.at[pl.ds(pl.multiple_of(row0, MOE_BLOCK), MOE_BLOCK)], zsem)

        def start(e, c):
            @pl.when(pend_ref[e] > pstart_ref[e])
            def _():
                zero_block(pend_ref[e] - MOE_BLOCK).start()
            return c

        def wait(e, c):
            @pl.when(pend_ref[e] > pstart_ref[e])
            def _():
                zero_block(0).wait()
            return c

        def tail_start(j, c):
            zero_block(pend_ref[n_exp - 1] + j * MOE_BLOCK).start()
            return c

        def tail_wait(j, c):
            zero_block(0).wait()
            return c

        lax.fori_loop(0, n_exp, start, 0)
        lax.fori_loop(0, n_tail, tail_start, 0)
        lax.fori_loop(0, n_exp, wait, 0)
        lax.fori_loop(0, n_tail, tail_wait, 0)

    def start(t, c):
        for k in range(TOP_K):
            dst = pstart_ref[eidx_ref[k, t]] + pos_ref[k, t]
            _row_copy(h2p_ref, t, xs_ref, dst, sem).start(priority=k % N_DMA_QUEUES)
        return c

    def wait(t, c):
        for k in range(TOP_K):
            _row_copy(h2p_ref, t, xs_ref, 0, sem).wait()
        return c

    lax.fori_loop(0, tm, start, 0)
    lax.fori_loop(0, tm, wait, 0)


def _dispatch_call(pstart, pend, eidx, pos, h2p, n_rows):
    N, W = h2p.shape
    tm = min(TM_DISP, N)
    grid_spec = pltpu.PrefetchScalarGridSpec(
        num_scalar_prefetch=2,
        grid=(N // tm,),
        in_specs=[pl.BlockSpec((TOP_K, tm), lambda i, *_: (0, i), memory_space=pltpu.SMEM),
                  pl.BlockSpec((TOP_K, tm), lambda i, *_: (0, i), memory_space=pltpu.SMEM),
                  pl.BlockSpec((tm, W), lambda i, *_: (i, 0))],
        out_specs=pl.BlockSpec(memory_space=pl.ANY),
        scratch_shapes=[pltpu.VMEM((MOE_BLOCK, W), h2p.dtype),
                        pltpu.SemaphoreType.DMA(()),
                        pltpu.SemaphoreType.DMA(())],
    )
    return pl.pallas_call(
        _dispatch_kernel,
        grid_spec=grid_spec,
        out_shape=jax.ShapeDtypeStruct((n_rows, W), h2p.dtype),
        compiler_params=_cparams(1),
        name="dispatch",
    )(pstart, pend, eidx, pos, h2p)


def _experts_kernel(pstart_ref, pend_ref, xs_ref, wg_ref, wu_ref, wd_ref, ys_ref,
                    xbuf, ybuf, wgu_s, wd_s, xsem, ysem):
    e = pl.program_id(0)
    n_exp = pl.num_programs(0)
    hd = wg_ref.shape[2]
    M = xbuf.shape[1]
    n_used = pend_ref[n_exp - 1] // M
    n_total = xs_ref.shape[0] // M

    def rows(b):
        return pl.ds(pl.multiple_of(b * M, M), M)

    def x_copy(b, slot):
        return pltpu.make_async_copy(xs_ref.at[rows(b)], xbuf.at[slot], xsem.at[slot])

    def y_copy(b, slot):
        return pltpu.make_async_copy(ybuf.at[slot], ys_ref.at[rows(b)], ysem.at[slot])

    @pl.when((e == 0) & (n_used > 0))
    def _():
        x_copy(0, 0).start()

    b0 = pstart_ref[e] // M
    nb = pend_ref[e] // M - b0

    @pl.when(nb > 0)
    def _():
        wgu_s[:, 0:hd] = wg_ref[0].astype(BF16)
        wgu_s[:, hd:2 * hd] = wu_ref[0].astype(BF16)
        wd_s[...] = wd_ref[0].astype(BF16)

    def block(j, c):
        b = b0 + j
        slot = b & 1
        x_copy(b, slot).wait()

        @pl.when(b + 1 < n_used)
        def _():
            x_copy(b + 1, 1 - slot).start()

        h = _dot(xbuf[slot].astype(BF16), wgu_s[...])
        hid = (_silu(h[:, 0:hd]) * h[:, hd:2 * hd]).astype(BF16)
        y = _dot(hid, wd_s[...])

        @pl.when(b >= 2)
        def _():
            y_copy(b - 2, slot).wait()

        ybuf[slot] = y
        y_copy(b, slot).start()
        return c

    lax.fori_loop(0, nb, block, 0)

    @pl.when(e == n_exp - 1)
    def _():
        @pl.when(n_used >= 2)
        def _():
            y_copy(n_used - 2, n_used & 1).wait()

        @pl.when(n_used >= 1)
        def _():
            y_copy(n_used - 1, (n_used - 1) & 1).wait()

        ybuf[0] = jnp.zeros(ybuf.shape[1:], ybuf.dtype)

        def tail_start(j, c):
            y_copy(n_used + j, 0).start()
            return c

        def tail_wait(j, c):
            y_copy(n_used + j, 0).wait()
            return c

        lax.fori_loop(0, n_total - n_used, tail_start, 0)
        lax.fori_loop(0, n_total - n_used, tail_wait, 0)


def _experts_call(pstart, pend, xs, w_gate, w_up, w_down):
    n_rows, D = xs.shape
    E, _, Hd = w_gate.shape
    wmap = lambda e, *_: (e, 0, 0)
    grid_spec = pltpu.PrefetchScalarGridSpec(
        num_scalar_prefetch=2,
        grid=(E,),
        in_specs=[pl.BlockSpec(memory_space=pl.ANY),
                  pl.BlockSpec((1, D, Hd), wmap),
                  pl.BlockSpec((1, D, Hd), wmap),
                  pl.BlockSpec((1, Hd, D), wmap)],
        out_specs=pl.BlockSpec(memory_space=pl.ANY),
        scratch_shapes=[pltpu.VMEM((2, MOE_BLOCK, D), xs.dtype),
                        pltpu.VMEM((2, MOE_BLOCK, D), F32),
                        pltpu.VMEM((D, 2 * Hd), BF16),
                        pltpu.VMEM((Hd, D), BF16),
                        pltpu.SemaphoreType.DMA((2,)),
                        pltpu.SemaphoreType.DMA((2,))],
    )
    return pl.pallas_call(
        _experts_kernel,
        grid_spec=grid_spec,
        out_shape=jax.ShapeDtypeStruct((n_rows, D), F32),
        compiler_params=_cparams(1),
        name="experts",
    )(pstart, pend, xs, w_gate, w_up, w_down)


def _combine_kernel(pstart_ref, eidx_ref, pos_ref, ys_ref, wgt_ref, h2_ref, x1_ref, ga_ref,
                    g2_ref, b2_ref, wsg_ref, wsu_ref, wsd_ref, o_ref, buf_ref, sem, *, alpha):
    tm = h2_ref.shape[1]

    def start(t, c):
        for k in range(TOP_K):
            src = pstart_ref[eidx_ref[k, t]] + pos_ref[k, t]
            _row_copy(ys_ref, src, buf_ref.at[k], t, sem).start(priority=k % N_DMA_QUEUES)
        return c

    def wait(t, c):
        for k in range(TOP_K):
            _row_copy(ys_ref, 0, buf_ref.at[k], t, sem).wait()
        return c

    lax.fori_loop(0, tm, start, 0)
    hb = h2_ref[0].astype(BF16)
    hid = (_silu(_dot(hb, wsg_ref[...])) * _dot(hb, wsu_ref[...])).astype(BF16)
    y = _dot(hid, wsd_ref[...])
    lax.fori_loop(0, tm, wait, 0)
    wgt = wgt_ref[...]
    for k in range(TOP_K):
        y = y + buf_ref[k] * wgt[:, k:k + 1]
    o_ref[0] = _normalize_rows(alpha * x1_ref[0] + (1.0 + ga_ref[0]) * y) * g2_ref[...] + b2_ref[...]


def _combine_call(pstart, eidx, pos, ys, wgt_t, h2, x1, mod3, ln_g, ln_b, wsg, wsu, wsd, alpha):
    B, T, D = x1.shape
    tm = min(TM_COMB, T)
    nt = T // tm
    tokc = lambda b, i, *_: (0, b * nt + i)
    tokr = lambda b, i, *_: (b * nt + i, 0)
    row = lambda b, i, *_: (b, i, 0)
    const2 = lambda b, i, *_: (0, 0)
    grid_spec = pltpu.PrefetchScalarGridSpec(
        num_scalar_prefetch=1,
        grid=(B, nt),
        in_specs=[pl.BlockSpec((TOP_K, tm), tokc, memory_space=pltpu.SMEM),
                  pl.BlockSpec((TOP_K, tm), tokc, memory_space=pltpu.SMEM),
                  pl.BlockSpec(memory_space=pl.ANY),
                  pl.BlockSpec((tm, TOP_K), tokr),
                  pl.BlockSpec((1, tm, D), row),
                  pl.BlockSpec((1, tm, D), row),
                  pl.BlockSpec((1, 1, D), lambda b, i, *_: (b * 6 + 5, 0, 0)),
                  pl.BlockSpec((1, D), const2),
                  pl.BlockSpec((1, D), const2),
                  pl.BlockSpec(wsg.shape, const2),
                  pl.BlockSpec(wsu.shape, const2),
                  pl.BlockSpec(wsd.shape, const2)],
        out_specs=pl.BlockSpec((1, tm, D), row),
        scratch_shapes=[pltpu.VMEM((TOP_K, tm, D), F32),
                        pltpu.SemaphoreType.DMA(())],
    )
    return pl.pallas_call(
        functools.partial(_combine_kernel, alpha=alpha),
        grid_spec=grid_spec,
        out_shape=jax.ShapeDtypeStruct((B, T, D), F32),
        compiler_params=_cparams(2),
        name="combine",
    )(pstart, eidx, pos, ys, wgt_t, h2, x1, mod3, ln_g, ln_b, wsg, wsu, wsd)


def _pad_heads(w, n_heads, dh, side):
    D = w.shape[0]
    w = w.reshape(D, n_heads, dh)
    z = jnp.zeros_like(w)
    if side is None:
        out = jnp.concatenate([w, z], axis=-1)
    else:
        lo = jnp.concatenate([w, z], axis=-1)
        hi = jnp.concatenate([z, w], axis=-1)
        out = jnp.where(side[None, :, None], hi, lo)
    return out.reshape(D, n_heads * 2 * dh)


def _layout_w_in(w_in):
    D = w_in.shape[0]
    H, dh, G = NSA_HEADS, NSA_HEAD_DIM, NSA_KV_GROUPS
    o = 0
    wq = w_in[:, o:o + H * dh]; o += H * dh
    wcmp = w_in[:, o:o + 2 * G * dh]; o += 2 * G * dh
    wkvn = w_in[:, o:o + 4 * G * dh]; o += 4 * G * dh
    wng = w_in[:, o:o + H * 3]; o += H * 3
    wgq = w_in[:, o:o + GLA_HEADS * GLA_DK]; o += GLA_HEADS * GLA_DK
    wgk = w_in[:, o:o + GLA_HEADS * GLA_DK]; o += GLA_HEADS * GLA_DK
    wgv = w_in[:, o:o + GLA_HEADS * GLA_DV]; o += GLA_HEADS * GLA_DV
    wglr = w_in[:, o:o + GLA_GATE_RANK]; o += GLA_GATE_RANK
    wgr = w_in[:, o:o + GLA_HEADS * GLA_DV]; o += GLA_HEADS * GLA_DV
    high_half = (jnp.arange(H) // NSA_HPG) == 1
    wqn = _pad_heads(wq * (dh ** -0.5), H, dh, high_half)
    ng = wng.reshape(D, G, NSA_HPG * 3)
    wgate = jnp.concatenate([ng, jnp.zeros((D, G, LANES - NSA_HPG * 3), w_in.dtype)], -1).reshape(D, G * LANES)
    wgqk = jnp.concatenate([_pad_heads(wgq * (GLA_DK ** -0.5), GLA_HEADS, GLA_DK, None),
                            _pad_heads(wgk, GLA_HEADS, GLA_DK, None)], axis=1)
    wglr_p = jnp.concatenate([wglr, jnp.zeros((D, LANES - GLA_GATE_RANK), w_in.dtype)], axis=1)
    w_all = jnp.concatenate([wqn, wkvn, wcmp, wgate, wgqk, wgv, wglr_p, wgr], axis=1).astype(BF16)
    return w_all, wgv.T.astype(BF16)


def _overlap_matrix(n_cp, n_sp, seq_len):
    n_cmp = seq_len // CMP_STRIDE - CMP_BLOCK // CMP_STRIDE + 1
    n_slc = seq_len // SLC_BLOCK
    cs = np.arange(n_cmp) * CMP_STRIDE
    ce = cs + CMP_BLOCK - 1
    ss = np.arange(n_slc) * SLC_BLOCK
    se = ss + SLC_BLOCK - 1
    ov = np.clip(np.minimum(ce[:, None], se[None]) - np.maximum(cs[:, None], ss[None]) + 1, 0, None)
    out = np.zeros((n_cp, n_sp), np.float32)
    out[:n_cmp, :n_slc] = ov.astype(np.float32) / CMP_BLOCK
    return jnp.asarray(out, dtype=BF16)


def _round_up(a, m):
    return (a + m - 1) // m * m


def _token_mixer(x, mod3, w_in, cmp_pos_k, cmp_w1_k, cmp_w2_k, cmp_pos_v, cmp_w1_v, cmp_w2_v,
                 gla_w_a2, gla_b_a2, gla_norm_w):
    B, T, D = x.shape
    G, dh = NSA_KV_GROUPS, NSA_HEAD_DIM
    w_all, w_vt = _layout_w_in(w_in)
    qn, kvn, cmpkv, gates, gqk, gv, glr, gr, gvt = _inproj_call(x, mod3, w_all, w_vt)

    n_sub = T // CMP_STRIDE
    pos2 = jnp.tile(jnp.stack([cmp_pos_k, cmp_pos_v]), (1, 1, G))
    w1 = jnp.stack([cmp_w1_k, cmp_w1_v])
    w2 = jnp.stack([cmp_w2_k, cmp_w2_v])
    w2p = jnp.stack([jnp.pad(w2, ((0, 0), (0, 0), (g * dh, (G - 1 - g) * dh))) for g in range(G)], axis=1)
    kcvc = _compress_call(cmpkv, pos2, w1, w2p)
    n_cp = _round_up(n_sub, LANES)
    kcvc = jnp.pad(kcvc, ((0, 0), (0, 0), (0, n_cp - n_sub), (0, 0)))
    n_slc = T // SLC_BLOCK
    n_sp = _round_up(n_slc, LANES)
    cmp_last = np.arange(n_cp) * CMP_STRIDE + (CMP_BLOCK - 1)
    cfeat = _position_features(cmp_last, n_slc)
    cfeat[:, :n_slc] = 0.0
    kcx = jnp.concatenate([kcvc[0], jnp.broadcast_to(jnp.asarray(cfeat, BF16), (B, n_cp, LANES))], axis=2)
    kfeat = jnp.asarray(_position_features(np.arange(T), n_slc), BF16)
    o_nsa = _nsa_call(qn, gates, kcx, kcvc[1], kvn, kfeat, _overlap_matrix(n_cp, n_sp, T))

    wa = gla_w_a2.reshape(GLA_GATE_RANK, GLA_HEADS, GLA_DK)
    wa_pad = jnp.zeros((LANES, GLA_HEADS, LANES), F32).at[:GLA_GATE_RANK, :, :GLA_DK].set(wa)
    wa_pad = wa_pad.reshape(LANES, GLA_HEADS * LANES)
    ba_pad = jnp.zeros((GLA_HEADS, LANES), F32).at[:, :GLA_DK].set(gla_b_a2.reshape(GLA_HEADS, GLA_DK))
    ba_pad = ba_pad.reshape(1, GLA_HEADS * LANES)
    o_gla = _gla_call(gqk, gv, gvt, glr, gr, wa_pad, ba_pad, gla_norm_w.reshape(1, GLA_DV))
    return o_nsa, o_gla


def _layer(x, c_pad, l, alpha, w_mod, b_mod, w_in, cmp_pos_k, cmp_w1_k, cmp_w2_k, cmp_pos_v, cmp_w1_v,
           cmp_w2_v, gla_w_a2, gla_b_a2, gla_norm_w, w_o, ln1_g, ln1_b, w_router, router_bias,
           w_e_gate, w_e_up, w_e_down, w_s_gate, w_s_up, w_s_down, ln2_g, ln2_b):
    B, T, D = x.shape
    N = B * T
    E = N_EXPERTS
    mod = _mod_call(c_pad, w_mod[l], b_mod[l].reshape(1, -1))
    mod3 = mod[:B].reshape(B * 6, 1, D)

    o_nsa, o_gla = _token_mixer(x, mod3, w_in[l], cmp_pos_k[l], cmp_w1_k[l], cmp_w2_k[l], cmp_pos_v[l],
                                cmp_w1_v[l], cmp_w2_v[l], gla_w_a2[l], gla_b_a2[l], gla_norm_w[l])

    wo = w_o[l]
    n_nsa = NSA_HEADS * NSA_HEAD_DIM
    high_half = (jnp.arange(NSA_HEADS) // NSA_HPG) == 1
    wo_nsa = _pad_heads(wo[:n_nsa].T, NSA_HEADS, NSA_HEAD_DIM, high_half).T
    wo_pad = jnp.concatenate([wo_nsa, wo[n_nsa:]], axis=0).astype(BF16)
    wr_t = w_router[l].T
    wr_hi = wr_t.astype(BF16)
    wr_lo = (wr_t - wr_hi.astype(F32)).astype(BF16)
    tm_o = min(TM_OPROJ, T)
    tri = jnp.asarray(np.triu(np.ones((tm_o, tm_o), np.float32), 1), dtype=BF16)
    x1, h2, eidx, wgt, pos, counts = _oproj_call(
        o_nsa, o_gla, x, mod3, ln1_g[l].reshape(1, D), ln1_b[l].reshape(1, D), wo_pad, wr_hi, wr_lo,
        router_bias[l].reshape(E, 1), tri, alpha)

    cnt = counts[:, 0].astype(jnp.int32)
    padded = (cnt + MOE_BLOCK - 1) // MOE_BLOCK * MOE_BLOCK
    pend = jnp.cumsum(padded)
    pstart = pend - padded
    n_blocks = -(-N * TOP_K // MOE_BLOCK) + E

    xs = _dispatch_call(pstart, pend, eidx, pos, h2.reshape(N, D), n_blocks * MOE_BLOCK)
    ys = _experts_call(pstart, pend, xs, w_e_gate[l], w_e_up[l], w_e_down[l])
    return _combine_call(pstart, eidx, pos, ys, wgt.T, h2, x1, mod3, ln2_g[l].reshape(1, D),
                         ln2_b[l].reshape(1, D), w_s_gate[l].astype(BF16), w_s_up[l].astype(BF16),
                         w_s_down[l].astype(BF16), alpha)


def kernel(x, c, w_mod, b_mod, w_in, cmp_pos_k, cmp_w1_k, cmp_w2_k, cmp_pos_v, cmp_w1_v, cmp_w2_v, gla_w_a2, gla_b_a2, gla_norm_w, w_o, ln1_g, ln1_b, w_router, router_bias, w_e_gate, w_e_up, w_e_down, w_s_gate, w_s_up, w_s_down, ln2_g, ln2_b):
    depth = w_mod.shape[0]
    alpha = (2.0 * depth) ** 0.25
    B = x.shape[0]
    c_pad = jnp.pad(c, ((0, (-B) % 8), (0, 0)))
    for l in range(depth):
        x = _layer(x, c_pad, l, alpha, w_mod, b_mod, w_in, cmp_pos_k, cmp_w1_k, cmp_w2_k, cmp_pos_v,
                   cmp_w1_v, cmp_w2_v, gla_w_a2, gla_b_a2, gla_norm_w, w_o, ln1_g, ln1_b, w_router,
                   router_bias, w_e_gate, w_e_up, w_e_down, w_s_gate, w_s_up, w_s_down, ln2_g, ln2_b)
    return x
```

```python
import functools

import jax
import jax.numpy as jnp
import numpy as np
from jax import lax
from jax.experimental import pallas as pl
from jax.experimental.pallas import tpu as pltpu

NSA_HEADS = 8
NSA_KV_GROUPS = 2
NSA_HPG = NSA_HEADS // NSA_KV_GROUPS
NSA_HEAD_DIM = 64
CMP_BLOCK = 32
CMP_STRIDE = 16
CMP_HIDDEN = 256
SLC_BLOCK = 64
SLC_TOPN = 16
WINDOW = 512
FORCE_BONUS = 1e4
GLA_HEADS = 4
GLA_DK = 64
GLA_DV = 128
GLA_GATE_RANK = 16
GLA_TAU = 16.0
N_EXPERTS = 256
TOP_K = 8
N_EXPERT_GROUPS = 8
TOPK_GROUPS = 4
EXPERT_HIDDEN = 256
ROUTE_SCALE = 2.5
LN_EPS = 1e-5
NEG = -1e30
TAKEN = -3e38

LANES = 128
VMEM_LIMIT = 56 * 1024 * 1024
N_DMA_QUEUES = 2

TM_PROJ = 256
TQ_NSA = 256
TK_SLC = 512
GLA_CHUNK = 128
GLA_SUB = 16
GLA_EXP_CLAMP = 80.0
TM_OPROJ = 256
MOE_BLOCK = 128
TM_SLOTS = 2048
EXPERT_RING = 4
TM_DISP = 128
TM_COMB = 128

BF16 = jnp.bfloat16
F32 = jnp.float32


def _cparams(n_axes):
    return pltpu.CompilerParams(
        dimension_semantics=("arbitrary",) * n_axes, vmem_limit_bytes=VMEM_LIMIT)


def _dot(a, b):
    return jnp.dot(a, b, preferred_element_type=F32)


def _dot_nt(a, b):
    return lax.dot_general(a, b, (((1,), (1,)), ((), ())), preferred_element_type=F32)


def _split(a):
    hi = a.astype(BF16)
    lo = (a - hi.astype(F32)).astype(BF16)
    return hi, lo


def _dot3(a, b):
    ah, al = _split(a)
    bh, bl = _split(b)
    return _dot(ah, bh) + _dot(ah, bl) + _dot(al, bh)


def _silu(x):
    return x * (1.0 / (1.0 + jnp.exp(-x)))


def _sigmoid(x):
    return 1.0 / (1.0 + jnp.exp(-x))


def _normalize_rows(x):
    mu = jnp.mean(x, axis=-1, keepdims=True)
    xc = x - mu
    var = jnp.mean(xc * xc, axis=-1, keepdims=True)
    return xc * lax.rsqrt(var + LN_EPS)


def _mod_kernel(c_ref, w_ref, b_ref, o_ref):
    o_ref[...] = _dot3(_silu(c_ref[...]), w_ref[...]) + b_ref[...]


def _mod_call(c_pad, w_mod, b_mod):
    rows, d = c_pad.shape
    n_out = w_mod.shape[1]
    return pl.pallas_call(
        _mod_kernel,
        grid=(n_out // d,),
        in_specs=[pl.BlockSpec((rows, d), lambda j: (0, 0)),
                  pl.BlockSpec((d, d), lambda j: (0, j)),
                  pl.BlockSpec((1, d), lambda j: (0, j))],
        out_specs=pl.BlockSpec((rows, d), lambda j: (0, j)),
        out_shape=jax.ShapeDtypeStruct((rows, n_out), F32),
        compiler_params=_cparams(1),
        name="mod",
    )(c_pad, w_mod, b_mod)


_PROJ_COLS = (("qn", 1024, BF16), ("kvn", 512, BF16), ("cmp", 256, F32), ("gate", 256, F32),
              ("gqk", 1024, F32), ("gv", 512, BF16), ("glr", 128, F32), ("gr", 512, F32))


def _inproj_kernel(x_ref, sh_ref, sc_ref, w_ref, wvt_ref,
                   qn_ref, kvn_ref, cmp_ref, gate_ref, gqk_ref, gv_ref, glr_ref, gr_ref, gvt_ref):
    h = _normalize_rows(x_ref[0]) * (1.0 + sc_ref[0]) + sh_ref[0]
    hb = h.astype(BF16)
    outs = (qn_ref, kvn_ref, cmp_ref, gate_ref, gqk_ref, gv_ref, glr_ref, gr_ref)
    c0 = 0
    for (name, width, dt), o_ref in zip(_PROJ_COLS, outs):
        p = _dot(hb, w_ref[:, c0:c0 + width])
        if name == "gate":
            p = _sigmoid(p)
        o_ref[0] = p.astype(dt)
        c0 += width
    gvt_ref[0] = _dot_nt(wvt_ref[...], hb).astype(BF16)


def _inproj_call(x, mod3, w_all, w_vt):
    B, T, D = x.shape
    tm = min(TM_PROJ, T)
    ctot = w_all.shape[1]
    out_shape = [jax.ShapeDtypeStruct((B, T, wdt), dt) for _, wdt, dt in _PROJ_COLS]
    out_shape.append(jax.ShapeDtypeStruct((B, w_vt.shape[0], T), BF16))
    out_specs = [pl.BlockSpec((1, tm, wdt), lambda b, i: (b, i, 0)) for _, wdt, _ in _PROJ_COLS]
    out_specs.append(pl.BlockSpec((1, w_vt.shape[0], tm), lambda b, i: (b, 0, i)))
    return pl.pallas_call(
        _inproj_kernel,
        grid=(B, T // tm),
        in_specs=[pl.BlockSpec((1, tm, D), lambda b, i: (b, i, 0)),
                  pl.BlockSpec((1, 1, D), lambda b, i: (b * 6 + 0, 0, 0)),
                  pl.BlockSpec((1, 1, D), lambda b, i: (b * 6 + 1, 0, 0)),
                  pl.BlockSpec((D, ctot), lambda b, i: (0, 0)),
                  pl.BlockSpec(w_vt.shape, lambda b, i: (0, 0))],
        out_specs=out_specs,
        out_shape=out_shape,
        compiler_params=_cparams(2),
        name="inproj",
    )(x, mod3, mod3, w_all, w_vt)


def _compress_kernel(x_ref, pos_ref, w1_ref, w2_ref, o_ref):
    G, dh, st = NSA_KV_GROUPS, NSA_HEAD_DIM, CMP_STRIDE
    n_sub = o_ref.shape[2]
    zero_w = jnp.zeros((dh, w1_ref.shape[2]), BF16)
    out = jnp.zeros((n_sub, G * dh), F32)
    for g in range(G):
        halves = [jnp.zeros((n_sub, w1_ref.shape[2]), F32) for _ in range(CMP_BLOCK // st)]
        for l in range(st):
            x_l = x_ref[0, pl.ds(l, n_sub, stride=st), :]
            for half in range(len(halves)):
                r = half * st + l
                a = (x_l + pos_ref[0, r:r + 1, :]).astype(BF16)
                w = w1_ref[0, r * dh:(r + 1) * dh, :].astype(BF16)
                w = jnp.concatenate([w, zero_w] if g == 0 else [zero_w, w], axis=0)
                halves[half] = halves[half] + _dot(a, w)
        hid = halves[0] + pltpu.roll(halves[1], n_sub - 1, 0)
        out = out + _dot(_silu(hid).astype(BF16), w2_ref[0, g].astype(BF16))
    o_ref[0, 0] = out.astype(o_ref.dtype)


def _compress_call(cmpkv, pos2, w1, w2p):
    B, T, _ = cmpkv.shape
    n_sub = T // CMP_STRIDE
    return pl.pallas_call(
        _compress_kernel,
        grid=(2, B),
        in_specs=[pl.BlockSpec((1, T, LANES), lambda s, b: (b, 0, s)),
                  pl.BlockSpec((1,) + pos2.shape[1:], lambda s, b: (s, 0, 0)),
                  pl.BlockSpec((1,) + w1.shape[1:], lambda s, b: (s, 0, 0)),
                  pl.BlockSpec((1,) + w2p.shape[1:], lambda s, b: (s, 0, 0, 0))],
        out_specs=pl.BlockSpec((1, 1, n_sub, LANES), lambda s, b: (s, b, 0, 0)),
        out_shape=jax.ShapeDtypeStruct((2, B, n_sub, LANES), BF16),
        compiler_params=_cparams(2),
        name="compress",
    )(cmpkv, pos2, w1, w2p)


def _position_features(pos, n_slc):
    assert n_slc + 2 <= LANES and pos.max() // SLC_BLOCK < 256
    f = np.zeros((pos.shape[0], LANES), np.float32)
    blk = pos // SLC_BLOCK
    onehot = blk < n_slc
    f[np.arange(pos.shape[0])[onehot], blk[onehot]] = 1.0
    f[:, n_slc] = blk
    f[:, n_slc + 1] = pos % SLC_BLOCK
    return f


def _nsa_kernel(q_ref, gate_ref, kc_ref, vc_ref, kv_ref, kf_ref, ov_ref, o_ref,
                st_ref, kx_ref, kwx_ref, *, seq_len):
    g = pl.program_id(1)
    qi = pl.program_id(2)
    tq = q_ref.shape[1]
    P = NSA_HPG
    R = P * tq
    n_cp = kc_ref.shape[1]
    n_sp = ov_ref.shape[1]
    n_slc = seq_len // SLC_BLOCK
    n_sel = min(SLC_TOPN, n_slc)
    q0 = qi * tq

    @pl.when((g == 0) & (qi == 0))
    def _():
        kx_ref[:, 0:LANES] = kv_ref[0, :, 0:LANES]
        kx_ref[:, LANES:2 * LANES] = kf_ref[...]
        kwx_ref[:, 0:LANES] = kv_ref[0, :, 2 * LANES:3 * LANES]
        kwx_ref[:, LANES:2 * LANES] = kf_ref[...]

    qblk = q_ref[0]
    gscale = jnp.where(g == 0, 1.0, 2.0 ** (-P)).astype(F32)
    lane = lax.broadcasted_iota(jnp.int32, (1, LANES), 1)
    alibi_lanes = jnp.where(lane == n_slc, float(SLC_BLOCK), jnp.where(lane == n_slc + 1, 1.0, 0.0))
    q_heads = [qblk[:, p * LANES:(p + 1) * LANES] for p in range(P)]
    f_pos = [jnp.broadcast_to(alibi_lanes * (2.0 ** (-(p + 1)) * gscale), (tq, LANES)) for p in range(P)]
    Qa = jnp.concatenate([jnp.concatenate([q_heads[p], f_pos[p].astype(BF16)], axis=1)
                          for p in range(P)], axis=0)
    t_q = q0 + lax.broadcasted_iota(jnp.int32, (tq, 1), 0)
    t_row = jnp.concatenate([t_q] * P, axis=0)

    def heads(x):
        return jnp.concatenate([x] * P, axis=0)

    n_idx = lax.broadcasted_iota(jnp.int32, (1, n_cp), 1)
    ce = n_idx * CMP_STRIDE + (CMP_BLOCK - 1)
    s_c = _dot_nt(Qa, kc_ref[0]) + heads(jnp.where(ce <= t_q, 0.0, NEG))
    e_c = jnp.exp(s_c - jnp.max(s_c, axis=-1, keepdims=True))
    inv_c = jnp.where(t_row >= CMP_BLOCK - 1,
                      1.0 / jnp.maximum(jnp.sum(e_c, axis=-1, keepdims=True), 1e-30), 0.0)
    p_c = e_c * inv_c
    o_cmp = _dot(p_c.astype(BF16), vc_ref[0])

    p_sum = p_c[0:tq]
    for p in range(1, P):
        p_sum = p_sum + p_c[p * tq:(p + 1) * tq]
    ph, plo = _split(p_sum)
    imp = _dot(ph, ov_ref[...]) + _dot(plo, ov_ref[...])
    j_idx = lax.broadcasted_iota(jnp.int32, (1, n_sp), 1)
    t_blk = t_q // SLC_BLOCK
    causal_blk = j_idx <= t_blk
    forced = (j_idx == 0) | (j_idx == t_blk) | (j_idx == t_blk - 1)
    score = jnp.where(causal_blk, imp + jnp.where(forced, FORCE_BONUS, 0.0), NEG)
    st_ref[...] = score.T
    n_rk = _round_up(n_slc, 8)
    s_all = st_ref[0:n_rk, :]
    jrow = lax.broadcasted_iota(jnp.int32, (n_rk, tq), 0)

    def rank_body(jp, cnt):
        row = st_ref[pl.ds(jp, 1), :]
        ge = jnp.where(row >= s_all, 1.0, 0.0)
        gt = jnp.where(row > s_all, 1.0, 0.0)
        return cnt + jnp.where(jrow > jp, ge, gt)

    n_causal = (q0 + tq) // SLC_BLOCK
    cnt = lax.fori_loop(0, n_causal, rank_body, jnp.zeros((n_rk, tq), F32))
    unsel = jnp.where(cnt < n_sel, 0.0, NEG)
    if n_rk < n_sp:
        unsel = jnp.concatenate([unsel, jnp.zeros((n_sp - n_rk, tq), F32)], axis=0)
    sel_bias = unsel.T

    tk = min(TK_SLC, seq_len)
    Qs = jnp.concatenate([jnp.concatenate([q_heads[p], (sel_bias + f_pos[p]).astype(BF16)], axis=1)
                          for p in range(P)], axis=0)

    def slc_tile(k0, carry, bias):
        m_i, l_i, acc = carry
        s = _dot_nt(Qs, kx_ref[pl.ds(k0, tk), :])
        if bias is not None:
            s = s + bias
        m_new = jnp.maximum(m_i, jnp.max(s, axis=-1, keepdims=True))
        e = jnp.exp(s - m_new)
        alpha = jnp.exp(m_i - m_new)
        l_new = alpha * l_i + jnp.sum(e, axis=-1, keepdims=True)
        acc_new = alpha * acc + _dot(e.astype(BF16), kv_ref[0, pl.ds(k0, tk), LANES:2 * LANES])
        return m_new, l_new, acc_new

    n_full = q0 // tk
    init = (jnp.full((R, 1), NEG, F32), jnp.zeros((R, 1), F32), jnp.zeros((R, LANES), F32))
    carry = lax.fori_loop(0, n_full, lambda it, c: slc_tile(pl.multiple_of(it * tk, tk), c, None), init)
    k_last = pl.multiple_of(n_full * tk, tk)
    kpos = k_last + lax.broadcasted_iota(jnp.int32, (1, tk), 1)
    _, l_s, acc_s = slc_tile(k_last, carry, heads(jnp.where(kpos <= t_q, 0.0, NEG)))
    o_slc = acc_s / jnp.maximum(l_s, 1e-30)

    wlen = min(WINDOW + tq, seq_len)
    w0 = pl.multiple_of(jnp.minimum(jnp.maximum(q0 - WINDOW, 0), seq_len - wlen), tq)
    wpos = w0 + lax.broadcasted_iota(jnp.int32, (1, wlen), 1)
    win_bias = jnp.where(wpos <= t_q, jnp.where(wpos > t_q - WINDOW, 0.0, NEG), NEG)
    s_w = _dot_nt(Qa, kwx_ref[pl.ds(w0, wlen), :]) + heads(win_bias)
    e_w = jnp.exp(s_w - jnp.max(s_w, axis=-1, keepdims=True))
    o_win = _dot(e_w.astype(BF16), kv_ref[0, pl.ds(w0, wlen), 3 * LANES:4 * LANES])
    o_win = o_win / jnp.maximum(jnp.sum(e_w, axis=-1, keepdims=True), 1e-30)

    gates = gate_ref[0]
    lane = lax.broadcasted_iota(jnp.int32, (1, LANES), 1)
    live = (lane // NSA_HEAD_DIM) == g
    pieces = []
    for p in range(P):
        rows = slice(p * tq, (p + 1) * tq)
        o_p = (gates[:, 3 * p + 0:3 * p + 1] * o_cmp[rows]
               + gates[:, 3 * p + 1:3 * p + 2] * o_slc[rows]
               + gates[:, 3 * p + 2:3 * p + 3] * o_win[rows])
        pieces.append(jnp.where(live, o_p, 0.0))
    o_ref[0] = jnp.concatenate(pieces, axis=1).astype(o_ref.dtype)


def _nsa_call(qn, gates, kcx, vc, kvn, kfeat, overlap):
    B, T, _ = qn.shape
    G, P = NSA_KV_GROUPS, NSA_HPG
    tq = min(TQ_NSA, T)
    n_cp = kcx.shape[1]
    n_sp = overlap.shape[1]
    assert (T // SLC_BLOCK) % 8 == 0 and min(TK_SLC, T) % tq == 0
    return pl.pallas_call(
        functools.partial(_nsa_kernel, seq_len=T),
        grid=(B, G, T // tq),
        in_specs=[pl.BlockSpec((1, tq, P * LANES), lambda b, g, i: (b, i, g)),
                  pl.BlockSpec((1, tq, LANES), lambda b, g, i: (b, i, g)),
                  pl.BlockSpec((1, n_cp, 2 * LANES), lambda b, g, i: (b, 0, 0)),
                  pl.BlockSpec((1, n_cp, LANES), lambda b, g, i: (b, 0, 0)),
                  pl.BlockSpec((1, T, 4 * LANES), lambda b, g, i: (b, 0, 0)),
                  pl.BlockSpec((T, LANES), lambda b, g, i: (0, 0)),
                  pl.BlockSpec((n_cp, n_sp), lambda b, g, i: (0, 0))],
        out_specs=pl.BlockSpec((1, tq, P * LANES), lambda b, g, i: (b, i, g)),
        out_shape=jax.ShapeDtypeStruct((B, T, G * P * LANES), BF16),
        scratch_shapes=[pltpu.VMEM((n_sp, tq), F32),
                        pltpu.VMEM((T, 2 * LANES), BF16),
                        pltpu.VMEM((T, 2 * LANES), BF16)],
        compiler_params=_cparams(3),
        name="nsa",
    )(qn, gates, kcx, vc, kvn, kfeat, overlap)


def _gla_kernel(qk_ref, v_ref, vt_ref, glr_ref, r_ref, wa_ref, ba_ref, nw_ref, o_ref, st_ref):
    ci = pl.program_id(1)
    C = qk_ref.shape[1]
    H = GLA_HEADS
    n_sub = C // GLA_SUB

    @pl.when(ci == 0)
    def _():
        st_ref[...] = jnp.zeros_like(st_ref)

    z = _dot3(glr_ref[0], wa_ref[...]) + ba_ref[...]
    gdec = -(jnp.maximum(-z, 0.0) + jnp.log(1.0 + jnp.exp(-jnp.abs(z)))) * (1.0 / GLA_TAU)
    rowi = lax.broadcasted_iota(jnp.int32, (C, 1), 0)
    b = gdec
    sh = 1
    while sh < C:
        b = b + jnp.where(rowi >= sh, pltpu.roll(b, sh, 0), 0.0)
        sh *= 2

    qk = qk_ref[0]
    r_i = lax.broadcasted_iota(jnp.int32, (C, C), 0)
    c_i = lax.broadcasted_iota(jnp.int32, (C, C), 1)
    causal = c_i <= r_i
    for h in range(H):
        cols = slice(h * LANES, (h + 1) * LANES)
        qh = qk[:, h * LANES:(h + 1) * LANES]
        kh = qk[:, (H + h) * LANES:(H + h + 1) * LANES]
        bh = b[:, cols]
        bl = bh[C - 1:C, :]
        st = st_ref[h]
        o_inter = _dot_nt((qh * jnp.exp(bh)).astype(BF16), st.astype(BF16))
        a_rows = []
        for i in range(n_sub):
            rs = slice(i * GLA_SUB, (i + 1) * GLA_SUB)
            ref_b = bh[i * GLA_SUB:i * GLA_SUB + 1, :]
            q_i = (qh[rs] * jnp.exp(bh[rs] - ref_b)).astype(BF16)
            k_i = (kh * jnp.exp(jnp.minimum(ref_b - bh, GLA_EXP_CLAMP))).astype(BF16)
            a_rows.append(_dot_nt(q_i, k_i))
        attn = jnp.where(causal, jnp.concatenate(a_rows, axis=0), 0.0)
        o_h = o_inter + _dot(attn.astype(BF16), v_ref[0, :, cols])
        kd = (kh * jnp.exp(bl - bh)).astype(BF16)
        st_ref[h] = st * jnp.exp(bl) + _dot(vt_ref[0, h * GLA_DV:(h + 1) * GLA_DV, :], kd)
        o_n = _normalize_rows(o_h) * nw_ref[...]
        o_ref[0, :, cols] = (o_n * _silu(r_ref[0, :, cols])).astype(o_ref.dtype)


def _gla_call(gqk, gv, gvt, glr, gr, wa_pad, ba_pad, norm_w):
    B, T, _ = gqk.shape
    H = GLA_HEADS
    C = min(GLA_CHUNK, T)
    return pl.pallas_call(
        _gla_kernel,
        grid=(B, T // C),
        in_specs=[pl.BlockSpec((1, C, 2 * H * LANES), lambda b, i: (b, i, 0)),
                  pl.BlockSpec((1, C, H * GLA_DV), lambda b, i: (b, i, 0)),
                  pl.BlockSpec((1, H * GLA_DV, C), lambda b, i: (b, 0, i)),
                  pl.BlockSpec((1, C, LANES), lambda b, i: (b, i, 0)),
                  pl.BlockSpec((1, C, H * GLA_DV), lambda b, i: (b, i, 0)),
                  pl.BlockSpec(wa_pad.shape, lambda b, i: (0, 0)),
                  pl.BlockSpec(ba_pad.shape, lambda b, i: (0, 0)),
                  pl.BlockSpec(norm_w.shape, lambda b, i: (0, 0))],
        out_specs=pl.BlockSpec((1, C, H * GLA_DV), lambda b, i: (b, i, 0)),
        out_shape=jax.ShapeDtypeStruct((B, T, H * GLA_DV), BF16),
        scratch_shapes=[pltpu.VMEM((H, GLA_DV, LANES), F32)],
        compiler_params=_cparams(2),
        name="gla",
    )(gqk, gv, gvt, glr, gr, wa_pad, ba_pad, norm_w)


def _oproj_kernel(on_ref, og_ref, x_ref, ga_ref, sh_ref, sc_ref, g1_ref, b1_ref, wo_ref,
                  wrh_ref, wrl_ref, rb_ref, tri_ref,
                  x1_ref, h2_ref, eidx_ref, wgt_ref, pos_ref, cnt_ref, run_ref, *, alpha):
    first = (pl.program_id(0) == 0) & (pl.program_id(1) == 0)

    @pl.when(first)
    def _():
        run_ref[...] = jnp.zeros_like(run_ref)

    kn = on_ref.shape[2]
    y = _dot(on_ref[0], wo_ref[0:kn, :]) + _dot(og_ref[0], wo_ref[kn:, :])
    x1 = _normalize_rows(alpha * x_ref[0] + (1.0 + ga_ref[0]) * y) * g1_ref[...] + b1_ref[...]
    x1_ref[0] = x1
    h2 = _normalize_rows(x1) * (1.0 + sc_ref[0]) + sh_ref[0]
    h2_ref[...] = h2
    hh, hl = _split(h2)

    logit = _dot_nt(wrh_ref[...], hh) + _dot_nt(wrh_ref[...], hl) + _dot_nt(wrl_ref[...], hh)
    s = _sigmoid(logit)
    sb = s + rb_ref[...]
    E, tm = s.shape
    gsz = E // N_EXPERT_GROUPS
    gi = lax.broadcasted_iota(jnp.int32, (gsz, tm), 0).astype(F32)
    gscore = []
    for gidx in range(N_EXPERT_GROUPS):
        blk = sb[gidx * gsz:(gidx + 1) * gsz]
        m1 = jnp.max(blk, axis=0, keepdims=True)
        i1 = jnp.min(jnp.where(blk == m1, gi, float(gsz)), axis=0, keepdims=True)
        m2 = jnp.max(jnp.where(gi == i1, TAKEN, blk), axis=0, keepdims=True)
        gscore.append(m1 + m2)
    masked = []
    for gidx in range(N_EXPERT_GROUPS):
        rank = jnp.zeros((1, tm), F32)
        for other in range(N_EXPERT_GROUPS):
            if other == gidx:
                continue
            beats = (gscore[other] >= gscore[gidx]) if other < gidx else (gscore[other] > gscore[gidx])
            rank = rank + jnp.where(beats, 1.0, 0.0)
        masked.append(jnp.where(rank < TOPK_GROUPS, sb[gidx * gsz:(gidx + 1) * gsz], NEG))
    ms = jnp.concatenate(masked, axis=0)
    ei = lax.broadcasted_iota(jnp.int32, (E, tm), 0).astype(F32)
    picks, idxs, vals = [], [], []
    for _ in range(TOP_K):
        m = jnp.max(ms, axis=0, keepdims=True)
        ix = jnp.min(jnp.where(ms == m, ei, float(E)), axis=0, keepdims=True)
        pick = ei == ix
        ms = jnp.where(pick, TAKEN, ms)
        picks.append(pick)
        idxs.append(ix)
        vals.append(jnp.sum(jnp.where(pick, s, 0.0), axis=0, keepdims=True))
    wsum = vals[0]
    for v in vals[1:]:
        wsum = wsum + v
    sel = jnp.zeros((E, tm), F32)
    for pick in picks:
        sel = sel + jnp.where(pick, 1.0, 0.0)
    pos_all = run_ref[:, 0:1] + _dot(sel.astype(BF16), tri_ref[...])
    run_ref[...] = run_ref[...] + jnp.sum(sel, axis=1, keepdims=True)
    cnt_ref[...] = run_ref[...]
    for k in range(TOP_K):
        eidx_ref[k:k + 1, :] = idxs[k].astype(jnp.int32)
        wgt_ref[k:k + 1, :] = vals[k] / wsum * ROUTE_SCALE
        pos_ref[k:k + 1, :] = jnp.sum(jnp.where(picks[k], pos_all, 0.0), axis=0,
                                      keepdims=True).astype(jnp.int32)


def _oproj_call(o_nsa, o_gla, x, mod3, ln_g, ln_b, wo_pad, wr_hi, wr_lo, rbias, tri, alpha):
    B, T, D = x.shape
    tm = min(TM_OPROJ, T)
    N = B * T
    E = wr_hi.shape[0]
    nt = T // tm
    tok = lambda b, i: (0, b * nt + i)
    const2 = lambda b, i: (0, 0)
    return pl.pallas_call(
        functools.partial(_oproj_kernel, alpha=alpha),
        grid=(B, nt),
        in_specs=[pl.BlockSpec((1, tm, o_nsa.shape[2]), lambda b, i: (b, i, 0)),
                  pl.BlockSpec((1, tm, o_gla.shape[2]), lambda b, i: (b, i, 0)),
                  pl.BlockSpec((1, tm, D), lambda b, i: (b, i, 0)),
                  pl.BlockSpec((1, 1, D), lambda b, i: (b * 6 + 2, 0, 0)),
                  pl.BlockSpec((1, 1, D), lambda b, i: (b * 6 + 3, 0, 0)),
                  pl.BlockSpec((1, 1, D), lambda b, i: (b * 6 + 4, 0, 0)),
                  pl.BlockSpec((1, D), const2),
                  pl.BlockSpec((1, D), const2),
                  pl.BlockSpec(wo_pad.shape, const2),
                  pl.BlockSpec(wr_hi.shape, const2),
                  pl.BlockSpec(wr_lo.shape, const2),
                  pl.BlockSpec(rbias.shape, const2),
                  pl.BlockSpec(tri.shape, const2)],
        out_specs=[pl.BlockSpec((1, tm, D), lambda b, i: (b, i, 0)),
                   pl.BlockSpec((tm, D), lambda b, i: (b * nt + i, 0)),
                   pl.BlockSpec((TOP_K, tm), tok),
                   pl.BlockSpec((TOP_K, tm), tok),
                   pl.BlockSpec((TOP_K, tm), tok),
                   pl.BlockSpec((E, LANES), const2)],
        out_shape=[jax.ShapeDtypeStruct((B, T, D), F32),
                   jax.ShapeDtypeStruct((N, D), F32),
                   jax.ShapeDtypeStruct((TOP_K, N), jnp.int32),
                   jax.ShapeDtypeStruct((TOP_K, N), F32),
                   jax.ShapeDtypeStruct((TOP_K, N), jnp.int32),
                   jax.ShapeDtypeStruct((E, LANES), F32)],
        scratch_shapes=[pltpu.VMEM((E, LANES), F32)],
        compiler_params=_cparams(2),
        name="oproj",
    )(o_nsa, o_gla, x, mod3, mod3, mod3, ln_g, ln_b, wo_pad, wr_hi, wr_lo, rbias, tri)


def _slots_kernel(pstart_ref, eidx_ref, pos_ref, dest_ref):
    eidx = eidx_ref[...]

    def body(e, acc):
        return acc + jnp.where(eidx == e, pstart_ref[e], 0)

    dest_ref[...] = lax.fori_loop(0, pstart_ref.shape[0], body, pos_ref[...])


def _slots_call(pstart, eidx, pos):
    K, N = eidx.shape
    tm = min(TM_SLOTS, N)
    grid_spec = pltpu.PrefetchScalarGridSpec(
        num_scalar_prefetch=1,
        grid=(N // tm,),
        in_specs=[pl.BlockSpec((K, tm), lambda i, *_: (0, i)),
                  pl.BlockSpec((K, tm), lambda i, *_: (0, i))],
        out_specs=pl.BlockSpec((K, tm), lambda i, *_: (0, i)),
    )
    return pl.pallas_call(
        _slots_kernel,
        grid_spec=grid_spec,
        out_shape=jax.ShapeDtypeStruct((K, N), jnp.int32),
        compiler_params=_cparams(1),
        name="slots",
    )(pstart, eidx, pos)


def _row_copy(src_ref, src_row, dst_ref, dst_row, sem):
    return pltpu.make_async_copy(src_ref.at[pl.ds(src_row, 1)], dst_ref.at[pl.ds(dst_row, 1)], sem)


def _dispatch_kernel(pstart_ref, pend_ref, dest_ref, h2p_ref, xs_ref, zero_ref, sem, zsem):
    step = pl.program_id(0)
    tm = h2p_ref.shape[0]
    n_exp = pstart_ref.shape[0]

    @pl.when(step == 0)
    def _():
        zero_ref[...] = jnp.zeros_like(zero_ref)
        n_tail = xs_ref.shape[0] // MOE_BLOCK - pend_ref[n_exp - 1] // MOE_BLOCK

        def zero_block(row0):
            return pltpu.make_async_copy(
                zero_ref, xs_ref.at[pl.ds(pl.multiple_of(row0, MOE_BLOCK), MOE_BLOCK)], zsem)

        def start(e, c):
            @pl.when(pend_ref[e] > pstart_ref[e])
            def _():
                zero_block(pend_ref[e] - MOE_BLOCK).start()
            return c

        def wait(e, c):
            @pl.when(pend_ref[e] > pstart_ref[e])
            def _():
                zero_block(0).wait()
            return c

        def tail_start(j, c):
            zero_block(pend_ref[n_exp - 1] + j * MOE_BLOCK).start()
            return c

        def tail_wait(j, c):
            zero_block(0).wait()
            return c

        lax.fori_loop(0, n_exp, start, 0)
        lax.fori_loop(0, n_tail, tail_start, 0)
        lax.fori_loop(0, n_exp, wait, 0)
        lax.fori_loop(0, n_tail, tail_wait, 0)

    def start(t, c):
        for k in range(TOP_K):
            _row_copy(h2p_ref, t, xs_ref, dest_ref[k, t], sem).start(priority=k % N_DMA_QUEUES)
        return c

    def wait(t, c):
        for k in range(TOP_K):
            _row_copy(h2p_ref, t, xs_ref, 0, sem).wait()
        return c

    lax.fori_loop(0, tm, start, 0)
    lax.fori_loop(0, tm, wait, 0)


def _dispatch_call(pstart, pend, dest, h2p, n_rows):
    N = h2p.shape[0]
    tile = h2p.shape[1:]
    tm = min(TM_DISP, N)
    grid_spec = pltpu.PrefetchScalarGridSpec(
        num_scalar_prefetch=2,
        grid=(N // tm,),
        in_specs=[pl.BlockSpec((TOP_K, tm), lambda i, *_: (0, i), memory_space=pltpu.SMEM),
                  pl.BlockSpec((tm,) + tile, lambda i, *_: (i, 0))],
        out_specs=pl.BlockSpec(memory_space=pl.ANY),
        scratch_shapes=[pltpu.VMEM((MOE_BLOCK,) + tile, h2p.dtype),
                        pltpu.SemaphoreType.DMA(()),
                        pltpu.SemaphoreType.DMA(())],
    )
    return pl.pallas_call(
        _dispatch_kernel,
        grid_spec=grid_spec,
        out_shape=jax.ShapeDtypeStruct((n_rows,) + tile, h2p.dtype),
        compiler_params=_cparams(1),
        name="dispatch",
    )(pstart, pend, dest, h2p)


def _experts_kernel(pstart_ref, pend_ref, xs_ref, wg_ref, wu_ref, wd_ref, ys_ref,
                    xbuf, ybuf, wgu_s, wd_s, xsem, ysem):
    e = pl.program_id(0)
    n_exp = pl.num_programs(0)
    hd = wg_ref.shape[2]
    n_slots, M = xbuf.shape[0], xbuf.shape[1]
    assert n_slots & (n_slots - 1) == 0
    ahead = n_slots - 1
    n_used = pend_ref[n_exp - 1] // M
    n_total = xs_ref.shape[0] // M

    def rows(b):
        return pl.ds(pl.multiple_of(b * M, M), M)

    def x_copy(b):
        slot = b & (n_slots - 1)
        return pltpu.make_async_copy(xs_ref.at[rows(b)], xbuf.at[slot], xsem.at[slot])

    def y_copy(b):
        slot = b & (n_slots - 1)
        return pltpu.make_async_copy(ybuf.at[slot], ys_ref.at[rows(b)], ysem.at[slot])

    @pl.when(e == 0)
    def _():
        for j in range(ahead):
            @pl.when(j < n_used)
            def _():
                x_copy(j).start()

    b0 = pstart_ref[e] // M
    nb = pend_ref[e] // M - b0

    @pl.when(nb > 0)
    def _():
        wgu_s[:, 0:hd] = wg_ref[0].astype(BF16)
        wgu_s[:, hd:2 * hd] = wu_ref[0].astype(BF16)
        wd_s[...] = wd_ref[0].astype(BF16)

    def block(j, c):
        b = b0 + j
        slot = b & (n_slots - 1)
        x_copy(b).wait()

        @pl.when(b + ahead < n_used)
        def _():
            x_copy(b + ahead).start()

        xb = xbuf[slot].astype(BF16)
        h = _dot(xb, wgu_s[...])
        hid = (_silu(h[:, 0:hd]) * h[:, hd:2 * hd]).astype(BF16)
        y = _dot(hid, wd_s[...])

        @pl.when(b >= n_slots)
        def _():
            y_copy(b - n_slots).wait()

        ybuf[slot] = y
        y_copy(b).start()
        return c

    lax.fori_loop(0, nb, block, 0)

    @pl.when(e == n_exp - 1)
    def _():
        for j in range(1, n_slots + 1):
            @pl.when(n_used >= j)
            def _():
                y_copy(n_used - j).wait()

        ybuf[0] = jnp.zeros(ybuf.shape[1:], ybuf.dtype)

        def tail_copy(j):
            return pltpu.make_async_copy(ybuf.at[0], ys_ref.at[rows(n_used + j)], ysem.at[0])

        def tail_start(j, c):
            tail_copy(j).start()
            return c

        def tail_wait(j, c):
            tail_copy(j).wait()
            return c

        lax.fori_loop(0, n_total - n_used, tail_start, 0)
        lax.fori_loop(0, n_total - n_used, tail_wait, 0)


def _experts_call(pstart, pend, xs, w_gate, w_up, w_down):
    n_rows = xs.shape[0]
    tile = xs.shape[1:]
    E, D, Hd = w_gate.shape
    wmap = lambda e, *_: (e, 0, 0)
    grid_spec = pltpu.PrefetchScalarGridSpec(
        num_scalar_prefetch=2,
        grid=(E,),
        in_specs=[pl.BlockSpec(memory_space=pl.ANY),
                  pl.BlockSpec((1, D, Hd), wmap),
                  pl.BlockSpec((1, D, Hd), wmap),
                  pl.BlockSpec((1, Hd, D), wmap)],
        out_specs=pl.BlockSpec(memory_space=pl.ANY),
        scratch_shapes=[pltpu.VMEM((EXPERT_RING, MOE_BLOCK) + tile, xs.dtype),
                        pltpu.VMEM((EXPERT_RING, MOE_BLOCK) + tile, F32),
                        pltpu.VMEM((D, 2 * Hd), BF16),
                        pltpu.VMEM((Hd, D), BF16),
                        pltpu.SemaphoreType.DMA((EXPERT_RING,)),
                        pltpu.SemaphoreType.DMA((EXPERT_RING,))],
    )
    return pl.pallas_call(
        _experts_kernel,
        grid_spec=grid_spec,
        out_shape=jax.ShapeDtypeStruct((n_rows,) + tile, F32),
        compiler_params=_cparams(1),
        name="experts",
    )(pstart, pend, xs, w_gate, w_up, w_down)


def _combine_kernel(dest_ref, ys_ref, wgt_ref, h2_ref, x1_ref, ga_ref,
                    g2_ref, b2_ref, wsg_ref, wsu_ref, wsd_ref, o_ref, buf_ref, sem, *, alpha):
    tm = h2_ref.shape[0]

    def start(t, c):
        for k in range(TOP_K):
            _row_copy(ys_ref, dest_ref[k, t], buf_ref.at[k], t, sem).start(priority=k % N_DMA_QUEUES)
        return c

    def wait(t, c):
        for k in range(TOP_K):
            _row_copy(ys_ref, 0, buf_ref.at[k], t, sem).wait()
        return c

    lax.fori_loop(0, tm, start, 0)
    hb = h2_ref[...].astype(BF16)
    hid = (_silu(_dot(hb, wsg_ref[...])) * _dot(hb, wsu_ref[...])).astype(BF16)
    y = _dot(hid, wsd_ref[...])
    lax.fori_loop(0, tm, wait, 0)
    wgt = wgt_ref[...]
    for k in range(TOP_K):
        y = y + buf_ref[k] * wgt[:, k:k + 1]
    o_ref[0] = _normalize_rows(alpha * x1_ref[0] + (1.0 + ga_ref[0]) * y) * g2_ref[...] + b2_ref[...]


def _combine_call(dest, ys, wgt_t, h2, x1, mod3, ln_g, ln_b, wsg, wsu, wsd, alpha):
    B, T, D = x1.shape
    tile = ys.shape[1:]
    tm = min(TM_COMB, T)
    nt = T // tm
    tokc = lambda b, i: (0, b * nt + i)
    tokr = lambda b, i: (b * nt + i, 0)
    row = lambda b, i: (b, i, 0)
    const2 = lambda b, i: (0, 0)
    grid_spec = pltpu.PrefetchScalarGridSpec(
        num_scalar_prefetch=0,
        grid=(B, nt),
        in_specs=[pl.BlockSpec((TOP_K, tm), tokc, memory_space=pltpu.SMEM),
                  pl.BlockSpec(memory_space=pl.ANY),
                  pl.BlockSpec((tm, TOP_K), tokr),
                  pl.BlockSpec((tm,) + tile, lambda b, i: (b * nt + i, 0)),
                  pl.BlockSpec((1, tm, D), row),
                  pl.BlockSpec((1, 1, D), lambda b, i: (b * 6 + 5, 0, 0)),
                  pl.BlockSpec((1, D), const2),
                  pl.BlockSpec((1, D), const2),
                  pl.BlockSpec(wsg.shape, const2),
                  pl.BlockSpec(wsu.shape, const2),
                  pl.BlockSpec(wsd.shape, const2)],
        out_specs=pl.BlockSpec((1, tm, D), row),
        scratch_shapes=[pltpu.VMEM((TOP_K, tm) + tile, F32),
                        pltpu.SemaphoreType.DMA(())],
    )
    return pl.pallas_call(
        functools.partial(_combine_kernel, alpha=alpha),
        grid_spec=grid_spec,
        out_shape=jax.ShapeDtypeStruct((B, T, D), F32),
        compiler_params=_cparams(2),
        name="combine",
    )(dest, ys, wgt_t, h2, x1, mod3, ln_g, ln_b, wsg, wsu, wsd)


def _pad_heads(w, n_heads, dh, side):
    D = w.shape[0]
    w = w.reshape(D, n_heads, dh)
    z = jnp.zeros_like(w)
    if side is None:
        out = jnp.concatenate([w, z], axis=-1)
    else:
        lo = jnp.concatenate([w, z], axis=-1)
        hi = jnp.concatenate([z, w], axis=-1)
        out = jnp.where(side[None, :, None], hi, lo)
    return out.reshape(D, n_heads * 2 * dh)


def _layout_w_in(w_in):
    D = w_in.shape[0]
    H, dh, G = NSA_HEADS, NSA_HEAD_DIM, NSA_KV_GROUPS
    o = 0
    wq = w_in[:, o:o + H * dh]; o += H * dh
    wcmp = w_in[:, o:o + 2 * G * dh]; o += 2 * G * dh
    wkvn = w_in[:, o:o + 4 * G * dh]; o += 4 * G * dh
    wng = w_in[:, o:o + H * 3]; o += H * 3
    wgq = w_in[:, o:o + GLA_HEADS * GLA_DK]; o += GLA_HEADS * GLA_DK
    wgk = w_in[:, o:o + GLA_HEADS * GLA_DK]; o += GLA_HEADS * GLA_DK
    wgv = w_in[:, o:o + GLA_HEADS * GLA_DV]; o += GLA_HEADS * GLA_DV
    wglr = w_in[:, o:o + GLA_GATE_RANK]; o += GLA_GATE_RANK
    wgr = w_in[:, o:o + GLA_HEADS * GLA_DV]; o += GLA_HEADS * GLA_DV
    high_half = (jnp.arange(H) // NSA_HPG) == 1
    wqn = _pad_heads(wq * (dh ** -0.5), H, dh, high_half)
    ng = wng.reshape(D, G, NSA_HPG * 3)
    wgate = jnp.concatenate([ng, jnp.zeros((D, G, LANES - NSA_HPG * 3), w_in.dtype)], -1).reshape(D, G * LANES)
    wgqk = jnp.concatenate([_pad_heads(wgq * (GLA_DK ** -0.5), GLA_HEADS, GLA_DK, None),
                            _pad_heads(wgk, GLA_HEADS, GLA_DK, None)], axis=1)
    wglr_p = jnp.concatenate([wglr, jnp.zeros((D, LANES - GLA_GATE_RANK), w_in.dtype)], axis=1)
    w_all = jnp.concatenate([wqn, wkvn, wcmp, wgate, wgqk, wgv, wglr_p, wgr], axis=1).astype(BF16)
    return w_all, wgv.T.astype(BF16)


def _overlap_matrix(n_cp, n_sp, seq_len):
    n_cmp = seq_len // CMP_STRIDE - CMP_BLOCK // CMP_STRIDE + 1
    n_slc = seq_len // SLC_BLOCK
    cs = np.arange(n_cmp) * CMP_STRIDE
    ce = cs + CMP_BLOCK - 1
    ss = np.arange(n_slc) * SLC_BLOCK
    se = ss + SLC_BLOCK - 1
    ov = np.clip(np.minimum(ce[:, None], se[None]) - np.maximum(cs[:, None], ss[None]) + 1, 0, None)
    out = np.zeros((n_cp, n_sp), np.float32)
    out[:n_cmp, :n_slc] = ov.astype(np.float32) / CMP_BLOCK
    return jnp.asarray(out, dtype=BF16)


def _round_up(a, m):
    return (a + m - 1) // m * m


def _token_mixer(x, mod3, w_in, cmp_pos_k, cmp_w1_k, cmp_w2_k, cmp_pos_v, cmp_w1_v, cmp_w2_v,
                 gla_w_a2, gla_b_a2, gla_norm_w):
    B, T, D = x.shape
    G, dh = NSA_KV_GROUPS, NSA_HEAD_DIM
    w_all, w_vt = _layout_w_in(w_in)
    qn, kvn, cmpkv, gates, gqk, gv, glr, gr, gvt = _inproj_call(x, mod3, w_all, w_vt)

    n_sub = T // CMP_STRIDE
    pos2 = jnp.tile(jnp.stack([cmp_pos_k, cmp_pos_v]), (1, 1, G))
    w1 = jnp.stack([cmp_w1_k, cmp_w1_v])
    w2 = jnp.stack([cmp_w2_k, cmp_w2_v])
    w2p = jnp.stack([jnp.pad(w2, ((0, 0), (0, 0), (g * dh, (G - 1 - g) * dh))) for g in range(G)], axis=1)
    kcvc = _compress_call(cmpkv, pos2, w1, w2p)
    n_cp = _round_up(n_sub, LANES)
    kcvc = jnp.pad(kcvc, ((0, 0), (0, 0), (0, n_cp - n_sub), (0, 0)))
    n_slc = T // SLC_BLOCK
    n_sp = _round_up(n_slc, LANES)
    cmp_last = np.arange(n_cp) * CMP_STRIDE + (CMP_BLOCK - 1)
    cfeat = _position_features(cmp_last, n_slc)
    cfeat[:, :n_slc] = 0.0
    kcx = jnp.concatenate([kcvc[0], jnp.broadcast_to(jnp.asarray(cfeat, BF16), (B, n_cp, LANES))], axis=2)
    kfeat = jnp.asarray(_position_features(np.arange(T), n_slc), BF16)
    o_nsa = _nsa_call(qn, gates, kcx, kcvc[1], kvn, kfeat, _overlap_matrix(n_cp, n_sp, T))

    wa = gla_w_a2.reshape(GLA_GATE_RANK, GLA_HEADS, GLA_DK)
    wa_pad = jnp.zeros((LANES, GLA_HEADS, LANES), F32).at[:GLA_GATE_RANK, :, :GLA_DK].set(wa)
    wa_pad = wa_pad.reshape(LANES, GLA_HEADS * LANES)
    ba_pad = jnp.zeros((GLA_HEADS, LANES), F32).at[:, :GLA_DK].set(gla_b_a2.reshape(GLA_HEADS, GLA_DK))
    ba_pad = ba_pad.reshape(1, GLA_HEADS * LANES)
    o_gla = _gla_call(gqk, gv, gvt, glr, gr, wa_pad, ba_pad, gla_norm_w.reshape(1, GLA_DV))
    return o_nsa, o_gla


def _layer(x, c_pad, l, alpha, w_mod, b_mod, w_in, cmp_pos_k, cmp_w1_k, cmp_w2_k, cmp_pos_v, cmp_w1_v,
           cmp_w2_v, gla_w_a2, gla_b_a2, gla_norm_w, w_o, ln1_g, ln1_b, w_router, router_bias,
           w_e_gate, w_e_up, w_e_down, w_s_gate, w_s_up, w_s_down, ln2_g, ln2_b):
    B, T, D = x.shape
    N = B * T
    E = N_EXPERTS
    mod = _mod_call(c_pad, w_mod[l], b_mod[l].reshape(1, -1))
    mod3 = mod[:B].reshape(B * 6, 1, D)

    o_nsa, o_gla = _token_mixer(x, mod3, w_in[l], cmp_pos_k[l], cmp_w1_k[l], cmp_w2_k[l], cmp_pos_v[l],
                                cmp_w1_v[l], cmp_w2_v[l], gla_w_a2[l], gla_b_a2[l], gla_norm_w[l])

    wo = w_o[l]
    n_nsa = NSA_HEADS * NSA_HEAD_DIM
    high_half = (jnp.arange(NSA_HEADS) // NSA_HPG) == 1
    wo_nsa = _pad_heads(wo[:n_nsa].T, NSA_HEADS, NSA_HEAD_DIM, high_half).T
    wo_pad = jnp.concatenate([wo_nsa, wo[n_nsa:]], axis=0).astype(BF16)
    wr_t = w_router[l].T
    wr_hi = wr_t.astype(BF16)
    wr_lo = (wr_t - wr_hi.astype(F32)).astype(BF16)
    tm_o = min(TM_OPROJ, T)
    tri = jnp.asarray(np.triu(np.ones((tm_o, tm_o), np.float32), 1), dtype=BF16)
    x1, h2, eidx, wgt, pos, counts = _oproj_call(
        o_nsa, o_gla, x, mod3, ln1_g[l].reshape(1, D), ln1_b[l].reshape(1, D), wo_pad, wr_hi, wr_lo,
        router_bias[l].reshape(E, 1), tri, alpha)

    cnt = counts[:, 0].astype(jnp.int32)
    padded = (cnt + MOE_BLOCK - 1) // MOE_BLOCK * MOE_BLOCK
    pend = jnp.cumsum(padded)
    pstart = pend - padded
    n_blocks = -(-N * TOP_K // MOE_BLOCK) + E

    dest = _slots_call(pstart, eidx, pos)
    xs = _dispatch_call(pstart, pend, dest, h2, n_blocks * MOE_BLOCK)
    ys = _experts_call(pstart, pend, xs, w_e_gate[l], w_e_up[l], w_e_down[l])
    return _combine_call(dest, ys, wgt.T, h2, x1, mod3, ln2_g[l].reshape(1, D),
                         ln2_b[l].reshape(1, D), w_s_gate[l].astype(BF16), w_s_up[l].astype(BF16),
                         w_s_down[l].astype(BF16), alpha)


def kernel(x, c, w_mod, b_mod, w_in, cmp_pos_k, cmp_w1_k, cmp_w2_k, cmp_pos_v, cmp_w1_v, cmp_w2_v, gla_w_a2, gla_b_a2, gla_norm_w, w_o, ln1_g, ln1_b, w_router, router_bias, w_e_gate, w_e_up, w_e_down, w_s_gate, w_s_up, w_s_down, ln2_g, ln2_b):
    depth = w_mod.shape[0]
    alpha = (2.0 * depth) ** 0.25
    B = x.shape[0]
    c_pad = jnp.pad(c, ((0, (-B) % 8), (0, 0)))
    for l in range(depth):
        x = _layer(x, c_pad, l, alpha, w_mod, b_mod, w_in, cmp_pos_k, cmp_w1_k, cmp_w2_k, cmp_pos_v,
                   cmp_w1_v, cmp_w2_v, gla_w_a2, gla_b_a2, gla_norm_w, w_o, ln1_g, ln1_b, w_router,
                   router_bias, w_e_gate, w_e_up, w_e_down, w_s_gate, w_s_up, w_s_down, ln2_g, ln2_b)
    return x
```

```python
import functools

import jax
import jax.numpy as jnp
import numpy as np
from jax import lax
from jax.experimental import pallas as pl
from jax.experimental.pallas import tpu as pltpu

NSA_HEADS = 8
NSA_KV_GROUPS = 2
NSA_HPG = NSA_HEADS // NSA_KV_GROUPS
NSA_HEAD_DIM = 64
CMP_BLOCK = 32
CMP_STRIDE = 16
CMP_HIDDEN = 256
SLC_BLOCK = 64
SLC_TOPN = 16
WINDOW = 512
FORCE_BONUS = 1e4
GLA_HEADS = 4
GLA_DK = 64
GLA_DV = 128
GLA_GATE_RANK = 16
GLA_TAU = 16.0
N_EXPERTS = 256
TOP_K = 8
N_EXPERT_GROUPS = 8
TOPK_GROUPS = 4
EXPERT_HIDDEN = 256
ROUTE_SCALE = 2.5
LN_EPS = 1e-5
NEG = -1e30
TAKEN = -3e38

LANES = 128
VMEM_LIMIT = 56 * 1024 * 1024
N_DMA_QUEUES = 2

TM_PROJ = 256
TQ_NSA = 256
TK_SLC = 512
GLA_CHUNK = 128
GLA_SUB = 16
GLA_EXP_CLAMP = 80.0
TM_OPROJ = 256
MOE_BLOCK = 128
TM_SLOTS = 2048
EXPERT_RING = 8
TM_DISP = 128
TM_COMB = 128

BF16 = jnp.bfloat16
F32 = jnp.float32


def _cparams(n_axes):
    return pltpu.CompilerParams(
        dimension_semantics=("arbitrary",) * n_axes, vmem_limit_bytes=VMEM_LIMIT)


def _dot(a, b):
    return jnp.dot(a, b, preferred_element_type=F32)


def _dot_nt(a, b):
    return lax.dot_general(a, b, (((1,), (1,)), ((), ())), preferred_element_type=F32)


def _split(a):
    hi = a.astype(BF16)
    lo = (a - hi.astype(F32)).astype(BF16)
    return hi, lo


def _dot3(a, b):
    ah, al = _split(a)
    bh, bl = _split(b)
    return _dot(ah, bh) + _dot(ah, bl) + _dot(al, bh)


def _silu(x):
    return x * (1.0 / (1.0 + jnp.exp(-x)))


def _sigmoid(x):
    return 1.0 / (1.0 + jnp.exp(-x))


def _normalize_rows(x):
    mu = jnp.mean(x, axis=-1, keepdims=True)
    xc = x - mu
    var = jnp.mean(xc * xc, axis=-1, keepdims=True)
    return xc * lax.rsqrt(var + LN_EPS)


def _mod_kernel(c_ref, w_ref, b_ref, o_ref):
    o_ref[...] = _dot3(_silu(c_ref[...]), w_ref[...]) + b_ref[...]


def _mod_call(c_pad, w_mod, b_mod):
    rows, d = c_pad.shape
    n_out = w_mod.shape[1]
    return pl.pallas_call(
        _mod_kernel,
        grid=(n_out // d,),
        in_specs=[pl.BlockSpec((rows, d), lambda j: (0, 0)),
                  pl.BlockSpec((d, d), lambda j: (0, j)),
                  pl.BlockSpec((1, d), lambda j: (0, j))],
        out_specs=pl.BlockSpec((rows, d), lambda j: (0, j)),
        out_shape=jax.ShapeDtypeStruct((rows, n_out), F32),
        compiler_params=_cparams(1),
        name="mod",
    )(c_pad, w_mod, b_mod)


_PROJ_COLS = (("qn", 1024, BF16), ("kvn", 512, BF16), ("cmp", 256, F32), ("gate", 256, F32),
              ("gqk", 1024, F32), ("gv", 512, BF16), ("glr", 128, F32), ("gr", 512, F32))


def _inproj_kernel(x_ref, sh_ref, sc_ref, w_ref, wvt_ref,
                   qn_ref, kvn_ref, cmp_ref, gate_ref, gqk_ref, gv_ref, glr_ref, gr_ref, gvt_ref):
    h = _normalize_rows(x_ref[0]) * (1.0 + sc_ref[0]) + sh_ref[0]
    hb = h.astype(BF16)
    outs = (qn_ref, kvn_ref, cmp_ref, gate_ref, gqk_ref, gv_ref, glr_ref, gr_ref)
    c0 = 0
    for (name, width, dt), o_ref in zip(_PROJ_COLS, outs):
        p = _dot(hb, w_ref[:, c0:c0 + width])
        if name == "gate":
            p = _sigmoid(p)
        o_ref[0] = p.astype(dt)
        c0 += width
    gvt_ref[0] = _dot_nt(wvt_ref[...], hb).astype(BF16)


def _inproj_call(x, mod3, w_all, w_vt):
    B, T, D = x.shape
    tm = min(TM_PROJ, T)
    ctot = w_all.shape[1]
    out_shape = [jax.ShapeDtypeStruct((B, T, wdt), dt) for _, wdt, dt in _PROJ_COLS]
    out_shape.append(jax.ShapeDtypeStruct((B, w_vt.shape[0], T), BF16))
    out_specs = [pl.BlockSpec((1, tm, wdt), lambda b, i: (b, i, 0)) for _, wdt, _ in _PROJ_COLS]
    out_specs.append(pl.BlockSpec((1, w_vt.shape[0], tm), lambda b, i: (b, 0, i)))
    return pl.pallas_call(
        _inproj_kernel,
        grid=(B, T // tm),
        in_specs=[pl.BlockSpec((1, tm, D), lambda b, i: (b, i, 0)),
                  pl.BlockSpec((1, 1, D), lambda b, i: (b * 6 + 0, 0, 0)),
                  pl.BlockSpec((1, 1, D), lambda b, i: (b * 6 + 1, 0, 0)),
                  pl.BlockSpec((D, ctot), lambda b, i: (0, 0)),
                  pl.BlockSpec(w_vt.shape, lambda b, i: (0, 0))],
        out_specs=out_specs,
        out_shape=out_shape,
        compiler_params=_cparams(2),
        name="inproj",
    )(x, mod3, mod3, w_all, w_vt)


def _compress_kernel(x_ref, pos_ref, w1_ref, w2_ref, o_ref):
    G, dh, st = NSA_KV_GROUPS, NSA_HEAD_DIM, CMP_STRIDE
    n_sub = o_ref.shape[2]
    zero_w = jnp.zeros((dh, w1_ref.shape[2]), BF16)
    out = jnp.zeros((n_sub, G * dh), F32)
    for g in range(G):
        halves = [jnp.zeros((n_sub, w1_ref.shape[2]), F32) for _ in range(CMP_BLOCK // st)]
        for l in range(st):
            x_l = x_ref[0, pl.ds(l, n_sub, stride=st), :]
            for half in range(len(halves)):
                r = half * st + l
                a = (x_l + pos_ref[0, r:r + 1, :]).astype(BF16)
                w = w1_ref[0, r * dh:(r + 1) * dh, :].astype(BF16)
                w = jnp.concatenate([w, zero_w] if g == 0 else [zero_w, w], axis=0)
                halves[half] = halves[half] + _dot(a, w)
        hid = halves[0] + pltpu.roll(halves[1], n_sub - 1, 0)
        out = out + _dot(_silu(hid).astype(BF16), w2_ref[0, g].astype(BF16))
    o_ref[0, 0] = out.astype(o_ref.dtype)


def _compress_call(cmpkv, pos2, w1, w2p):
    B, T, _ = cmpkv.shape
    n_sub = T // CMP_STRIDE
    return pl.pallas_call(
        _compress_kernel,
        grid=(2, B),
        in_specs=[pl.BlockSpec((1, T, LANES), lambda s, b: (b, 0, s)),
                  pl.BlockSpec((1,) + pos2.shape[1:], lambda s, b: (s, 0, 0)),
                  pl.BlockSpec((1,) + w1.shape[1:], lambda s, b: (s, 0, 0)),
                  pl.BlockSpec((1,) + w2p.shape[1:], lambda s, b: (s, 0, 0, 0))],
        out_specs=pl.BlockSpec((1, 1, n_sub, LANES), lambda s, b: (s, b, 0, 0)),
        out_shape=jax.ShapeDtypeStruct((2, B, n_sub, LANES), BF16),
        compiler_params=_cparams(2),
        name="compress",
    )(cmpkv, pos2, w1, w2p)


def _position_features(pos, n_slc):
    assert n_slc + 2 <= LANES and pos.max() // SLC_BLOCK < 256
    f = np.zeros((pos.shape[0], LANES), np.float32)
    blk = pos // SLC_BLOCK
    onehot = blk < n_slc
    f[np.arange(pos.shape[0])[onehot], blk[onehot]] = 1.0
    f[:, n_slc] = blk
    f[:, n_slc + 1] = pos % SLC_BLOCK
    return f


def _nsa_kernel(q_ref, gate_ref, kc_ref, vc_ref, kv_ref, kf_ref, ov_ref, o_ref,
                st_ref, kx_ref, kwx_ref, tile_hit_ref, m_ref, l_ref, acc_ref, *, seq_len):
    g = pl.program_id(1)
    qi = pl.program_id(2)
    tq = q_ref.shape[1]
    P = NSA_HPG
    R = P * tq
    n_cp = kc_ref.shape[1]
    n_sp = ov_ref.shape[1]
    n_slc = seq_len // SLC_BLOCK
    n_sel = min(SLC_TOPN, n_slc)
    q0 = qi * tq

    @pl.when((g == 0) & (qi == 0))
    def _():
        kx_ref[:, 0:LANES] = kv_ref[0, :, 0:LANES]
        kx_ref[:, LANES:2 * LANES] = kf_ref[...]
        kwx_ref[:, 0:LANES] = kv_ref[0, :, 2 * LANES:3 * LANES]
        kwx_ref[:, LANES:2 * LANES] = kf_ref[...]

    qblk = q_ref[0]
    gscale = jnp.where(g == 0, 1.0, 2.0 ** (-P)).astype(F32)
    lane = lax.broadcasted_iota(jnp.int32, (1, LANES), 1)
    alibi_lanes = jnp.where(lane == n_slc, float(SLC_BLOCK), jnp.where(lane == n_slc + 1, 1.0, 0.0))
    q_heads = [qblk[:, p * LANES:(p + 1) * LANES] for p in range(P)]
    f_pos = [jnp.broadcast_to(alibi_lanes * (2.0 ** (-(p + 1)) * gscale), (tq, LANES)) for p in range(P)]
    Qa = jnp.concatenate([jnp.concatenate([q_heads[p], f_pos[p].astype(BF16)], axis=1)
                          for p in range(P)], axis=0)
    t_q = q0 + lax.broadcasted_iota(jnp.int32, (tq, 1), 0)
    t_row = jnp.concatenate([t_q] * P, axis=0)

    def heads(x):
        return jnp.concatenate([x] * P, axis=0)

    n_idx = lax.broadcasted_iota(jnp.int32, (1, n_cp), 1)
    ce = n_idx * CMP_STRIDE + (CMP_BLOCK - 1)
    s_c = _dot_nt(Qa, kc_ref[0]) + heads(jnp.where(ce <= t_q, 0.0, NEG))
    e_c = jnp.exp(s_c - jnp.max(s_c, axis=-1, keepdims=True))
    inv_c = jnp.where(t_row >= CMP_BLOCK - 1,
                      1.0 / jnp.maximum(jnp.sum(e_c, axis=-1, keepdims=True), 1e-30), 0.0)
    p_c = e_c * inv_c
    o_cmp = _dot(p_c.astype(BF16), vc_ref[0])

    p_sum = p_c[0:tq]
    for p in range(1, P):
        p_sum = p_sum + p_c[p * tq:(p + 1) * tq]
    ph, plo = _split(p_sum)
    imp = _dot(ph, ov_ref[...]) + _dot(plo, ov_ref[...])
    j_idx = lax.broadcasted_iota(jnp.int32, (1, n_sp), 1)
    t_blk = t_q // SLC_BLOCK
    causal_blk = j_idx <= t_blk
    forced = (j_idx == 0) | (j_idx == t_blk) | (j_idx == t_blk - 1)
    score = jnp.where(causal_blk, imp + jnp.where(forced, FORCE_BONUS, 0.0), NEG)
    st_ref[...] = score.T
    n_rk = _round_up(n_slc, 8)
    s_all = st_ref[0:n_rk, :]
    jrow = lax.broadcasted_iota(jnp.int32, (n_rk, tq), 0)

    def rank_body(jp, cnt):
        row = st_ref[pl.ds(jp, 1), :]
        ge = jnp.where(row >= s_all, 1.0, 0.0)
        gt = jnp.where(row > s_all, 1.0, 0.0)
        return cnt + jnp.where(jrow > jp, ge, gt)

    n_causal = (q0 + tq) // SLC_BLOCK
    cnt = lax.fori_loop(0, n_causal, rank_body, jnp.zeros((n_rk, tq), F32))
    unsel = jnp.where(cnt < n_sel, 0.0, NEG)
    tk = min(TK_SLC, seq_len)
    bpt = tk // SLC_BLOCK
    for i in range(n_slc // bpt):
        hit = jnp.max(jnp.where(cnt[i * bpt:(i + 1) * bpt] < n_sel, 1.0, 0.0))
        tile_hit_ref[i] = (hit > 0.0).astype(jnp.int32)
    if n_rk < n_sp:
        unsel = jnp.concatenate([unsel, jnp.zeros((n_sp - n_rk, tq), F32)], axis=0)
    sel_bias = unsel.T

    Qs = jnp.concatenate([jnp.concatenate([q_heads[p], (sel_bias + f_pos[p]).astype(BF16)], axis=1)
                          for p in range(P)], axis=0)

    m_ref[...] = jnp.full((R, LANES), NEG, F32)
    l_ref[...] = jnp.zeros((R, LANES), F32)
    acc_ref[...] = jnp.zeros((R, LANES), F32)

    def slc_tile(k0, bias):
        s = _dot_nt(Qs, kx_ref[pl.ds(k0, tk), :])
        if bias is not None:
            s = s + bias
        m_i = m_ref[...]
        m_new = jnp.maximum(m_i, jnp.max(s, axis=-1, keepdims=True))
        e = jnp.exp(s - jnp.concatenate([m_new] * (tk // LANES), axis=1))
        alpha = jnp.exp(m_i - m_new)
        m_ref[...] = m_new
        l_ref[...] = alpha * l_ref[...] + jnp.sum(e, axis=-1, keepdims=True)
        acc_ref[...] = alpha * acc_ref[...] + _dot(e.astype(BF16), kv_ref[0, pl.ds(k0, tk), LANES:2 * LANES])

    n_full = q0 // tk

    def full_tile(it, c):
        @pl.when(tile_hit_ref[it] > 0)
        def _():
            slc_tile(pl.multiple_of(it * tk, tk), None)
        return c

    lax.fori_loop(0, n_full, full_tile, 0)
    k_last = pl.multiple_of(n_full * tk, tk)
    kpos = k_last + lax.broadcasted_iota(jnp.int32, (1, tk), 1)
    slc_tile(k_last, heads(jnp.where(kpos <= t_q, 0.0, NEG)))
    o_slc = acc_ref[...] / jnp.maximum(l_ref[...], 1e-30)

    wlen = min(WINDOW + tq, seq_len)
    w0 = pl.multiple_of(jnp.minimum(jnp.maximum(q0 - WINDOW, 0), seq_len - wlen), tq)
    wpos = w0 + lax.broadcasted_iota(jnp.int32, (1, wlen), 1)
    win_bias = jnp.where(wpos <= t_q, jnp.where(wpos > t_q - WINDOW, 0.0, NEG), NEG)
    s_w = _dot_nt(Qa, kwx_ref[pl.ds(w0, wlen), :]) + heads(win_bias)
    e_w = jnp.exp(s_w - jnp.max(s_w, axis=-1, keepdims=True))
    o_win = _dot(e_w.astype(BF16), kv_ref[0, pl.ds(w0, wlen), 3 * LANES:4 * LANES])
    o_win = o_win / jnp.maximum(jnp.sum(e_w, axis=-1, keepdims=True), 1e-30)

    gates = gate_ref[0]
    lane = lax.broadcasted_iota(jnp.int32, (1, LANES), 1)
    live = (lane // NSA_HEAD_DIM) == g
    pieces = []
    for p in range(P):
        rows = slice(p * tq, (p + 1) * tq)
        o_p = (gates[:, 3 * p + 0:3 * p + 1] * o_cmp[rows]
               + gates[:, 3 * p + 1:3 * p + 2] * o_slc[rows]
               + gates[:, 3 * p + 2:3 * p + 3] * o_win[rows])
        pieces.append(jnp.where(live, o_p, 0.0))
    o_ref[0] = jnp.concatenate(pieces, axis=1).astype(o_ref.dtype)


def _nsa_call(qn, gates, kcx, vc, kvn, kfeat, overlap):
    B, T, _ = qn.shape
    G, P = NSA_KV_GROUPS, NSA_HPG
    tq = min(TQ_NSA, T)
    n_cp = kcx.shape[1]
    n_sp = overlap.shape[1]
    assert (T // SLC_BLOCK) % 8 == 0 and min(TK_SLC, T) % tq == 0
    return pl.pallas_call(
        functools.partial(_nsa_kernel, seq_len=T),
        grid=(B, G, T // tq),
        in_specs=[pl.BlockSpec((1, tq, P * LANES), lambda b, g, i: (b, i, g)),
                  pl.BlockSpec((1, tq, LANES), lambda b, g, i: (b, i, g)),
                  pl.BlockSpec((1, n_cp, 2 * LANES), lambda b, g, i: (b, 0, 0)),
                  pl.BlockSpec((1, n_cp, LANES), lambda b, g, i: (b, 0, 0)),
                  pl.BlockSpec((1, T, 4 * LANES), lambda b, g, i: (b, 0, 0)),
                  pl.BlockSpec((T, LANES), lambda b, g, i: (0, 0)),
                  pl.BlockSpec((n_cp, n_sp), lambda b, g, i: (0, 0))],
        out_specs=pl.BlockSpec((1, tq, P * LANES), lambda b, g, i: (b, i, g)),
        out_shape=jax.ShapeDtypeStruct((B, T, G * P * LANES), BF16),
        scratch_shapes=[pltpu.VMEM((n_sp, tq), F32),
                        pltpu.VMEM((T, 2 * LANES), BF16),
                        pltpu.VMEM((T, 2 * LANES), BF16),
                        pltpu.SMEM((T // min(TK_SLC, T),), jnp.int32),
                        pltpu.VMEM((P * tq, LANES), F32),
                        pltpu.VMEM((P * tq, LANES), F32),
                        pltpu.VMEM((P * tq, LANES), F32)],
        compiler_params=_cparams(3),
        name="nsa",
    )(qn, gates, kcx, vc, kvn, kfeat, overlap)


def _gla_kernel(qk_ref, v_ref, vt_ref, glr_ref, r_ref, wa_ref, ba_ref, nw_ref, o_ref, st_ref):
    ci = pl.program_id(1)
    C = qk_ref.shape[1]
    H = GLA_HEADS
    n_sub = C // GLA_SUB

    @pl.when(ci == 0)
    def _():
        st_ref[...] = jnp.zeros_like(st_ref)

    z = _dot3(glr_ref[0], wa_ref[...]) + ba_ref[...]
    gdec = -(jnp.maximum(-z, 0.0) + jnp.log(1.0 + jnp.exp(-jnp.abs(z)))) * (1.0 / GLA_TAU)
    rowi = lax.broadcasted_iota(jnp.int32, (C, 1), 0)
    b = gdec
    sh = 1
    while sh < C:
        b = b + jnp.where(rowi >= sh, pltpu.roll(b, sh, 0), 0.0)
        sh *= 2

    qk = qk_ref[0]
    r_i = lax.broadcasted_iota(jnp.int32, (C, C), 0)
    c_i = lax.broadcasted_iota(jnp.int32, (C, C), 1)
    causal = c_i <= r_i
    for h in range(H):
        cols = slice(h * LANES, (h + 1) * LANES)
        qh = qk[:, h * LANES:(h + 1) * LANES]
        kh = qk[:, (H + h) * LANES:(H + h + 1) * LANES]
        bh = b[:, cols]
        bl = bh[C - 1:C, :]
        st = st_ref[h]
        o_inter = _dot_nt((qh * jnp.exp(bh)).astype(BF16), st.astype(BF16))
        a_rows = []
        for i in range(n_sub):
            rs = slice(i * GLA_SUB, (i + 1) * GLA_SUB)
            ref_b = bh[i * GLA_SUB:i * GLA_SUB + 1, :]
            q_i = (qh[rs] * jnp.exp(bh[rs] - ref_b)).astype(BF16)
            n_k = (i + 1) * GLA_SUB
            k_i = (kh[0:n_k] * jnp.exp(jnp.minimum(ref_b - bh[0:n_k], GLA_EXP_CLAMP))).astype(BF16)
            if n_k < C:
                k_i = jnp.concatenate([k_i, jnp.zeros((C - n_k, LANES), BF16)], axis=0)
            a_rows.append(_dot_nt(q_i, k_i))
        attn = jnp.where(causal, jnp.concatenate(a_rows, axis=0), 0.0)
        o_h = o_inter + _dot(attn.astype(BF16), v_ref[0, :, cols])
        kd = (kh * jnp.exp(bl - bh)).astype(BF16)
        st_ref[h] = st * jnp.exp(bl) + _dot(vt_ref[0, h * GLA_DV:(h + 1) * GLA_DV, :], kd)
        o_n = _normalize_rows(o_h) * nw_ref[...]
        o_ref[0, :, cols] = (o_n * _silu(r_ref[0, :, cols])).astype(o_ref.dtype)


def _gla_call(gqk, gv, gvt, glr, gr, wa_pad, ba_pad, norm_w):
    B, T, _ = gqk.shape
    H = GLA_HEADS
    C = min(GLA_CHUNK, T)
    return pl.pallas_call(
        _gla_kernel,
        grid=(B, T // C),
        in_specs=[pl.BlockSpec((1, C, 2 * H * LANES), lambda b, i: (b, i, 0)),
                  pl.BlockSpec((1, C, H * GLA_DV), lambda b, i: (b, i, 0)),
                  pl.BlockSpec((1, H * GLA_DV, C), lambda b, i: (b, 0, i)),
                  pl.BlockSpec((1, C, LANES), lambda b, i: (b, i, 0)),
                  pl.BlockSpec((1, C, H * GLA_DV), lambda b, i: (b, i, 0)),
                  pl.BlockSpec(wa_pad.shape, lambda b, i: (0, 0)),
                  pl.BlockSpec(ba_pad.shape, lambda b, i: (0, 0)),
                  pl.BlockSpec(norm_w.shape, lambda b, i: (0, 0))],
        out_specs=pl.BlockSpec((1, C, H * GLA_DV), lambda b, i: (b, i, 0)),
        out_shape=jax.ShapeDtypeStruct((B, T, H * GLA_DV), BF16),
        scratch_shapes=[pltpu.VMEM((H, GLA_DV, LANES), F32)],
        compiler_params=_cparams(2),
        name="gla",
    )(gqk, gv, gvt, glr, gr, wa_pad, ba_pad, norm_w)


def _oproj_kernel(on_ref, og_ref, x_ref, ga_ref, sh_ref, sc_ref, g1_ref, b1_ref, wo_ref,
                  wrh_ref, wrl_ref, rb_ref, tri_ref,
                  x1_ref, h2_ref, eidx_ref, wgt_ref, pos_ref, cnt_ref, run_ref, *, alpha):
    first = (pl.program_id(0) == 0) & (pl.program_id(1) == 0)

    @pl.when(first)
    def _():
        run_ref[...] = jnp.zeros_like(run_ref)

    kn = on_ref.shape[2]
    y = _dot(on_ref[0], wo_ref[0:kn, :]) + _dot(og_ref[0], wo_ref[kn:, :])
    x1 = _normalize_rows(alpha * x_ref[0] + (1.0 + ga_ref[0]) * y) * g1_ref[...] + b1_ref[...]
    x1_ref[0] = x1
    h2 = _normalize_rows(x1) * (1.0 + sc_ref[0]) + sh_ref[0]
    h2_ref[...] = h2
    hh, hl = _split(h2)

    logit = _dot_nt(wrh_ref[...], hh) + _dot_nt(wrh_ref[...], hl) + _dot_nt(wrl_ref[...], hh)
    s = _sigmoid(logit)
    sb = s + rb_ref[...]
    E, tm = s.shape
    gsz = E // N_EXPERT_GROUPS
    gi = lax.broadcasted_iota(jnp.int32, (gsz, tm), 0).astype(F32)
    gscore = []
    for gidx in range(N_EXPERT_GROUPS):
        blk = sb[gidx * gsz:(gidx + 1) * gsz]
        m1 = jnp.max(blk, axis=0, keepdims=True)
        i1 = jnp.min(jnp.where(blk == m1, gi, float(gsz)), axis=0, keepdims=True)
        m2 = jnp.max(jnp.where(gi == i1, TAKEN, blk), axis=0, keepdims=True)
        gscore.append(m1 + m2)
    masked = []
    for gidx in range(N_EXPERT_GROUPS):
        rank = jnp.zeros((1, tm), F32)
        for other in range(N_EXPERT_GROUPS):
            if other == gidx:
                continue
            beats = (gscore[other] >= gscore[gidx]) if other < gidx else (gscore[other] > gscore[gidx])
            rank = rank + jnp.where(beats, 1.0, 0.0)
        masked.append(jnp.where(rank < TOPK_GROUPS, sb[gidx * gsz:(gidx + 1) * gsz], NEG))
    ms = jnp.concatenate(masked, axis=0)
    ei = lax.broadcasted_iota(jnp.int32, (E, tm), 0).astype(F32)
    picks, idxs, vals = [], [], []
    for _ in range(TOP_K):
        m = jnp.max(ms, axis=0, keepdims=True)
        ix = jnp.min(jnp.where(ms == m, ei, float(E)), axis=0, keepdims=True)
        pick = ei == ix
        ms = jnp.where(pick, TAKEN, ms)
        picks.append(pick)
        idxs.append(ix)
        vals.append(jnp.sum(jnp.where(pick, s, 0.0), axis=0, keepdims=True))
    wsum = vals[0]
    for v in vals[1:]:
        wsum = wsum + v
    sel = jnp.zeros((E, tm), F32)
    for pick in picks:
        sel = sel + jnp.where(pick, 1.0, 0.0)
    pos_all = run_ref[:, 0:1] + _dot(sel.astype(BF16), tri_ref[...])
    run_ref[...] = run_ref[...] + jnp.sum(sel, axis=1, keepdims=True)
    cnt_ref[...] = run_ref[...]
    for k in range(TOP_K):
        eidx_ref[k:k + 1, :] = idxs[k].astype(jnp.int32)
        wgt_ref[k:k + 1, :] = vals[k] / wsum * ROUTE_SCALE
        pos_ref[k:k + 1, :] = jnp.sum(jnp.where(picks[k], pos_all, 0.0), axis=0,
                                      keepdims=True).astype(jnp.int32)


def _oproj_call(o_nsa, o_gla, x, mod3, ln_g, ln_b, wo_pad, wr_hi, wr_lo, rbias, tri, alpha):
    B, T, D = x.shape
    tm = min(TM_OPROJ, T)
    N = B * T
    E = wr_hi.shape[0]
    nt = T // tm
    tok = lambda b, i: (0, b * nt + i)
    const2 = lambda b, i: (0, 0)
    return pl.pallas_call(
        functools.partial(_oproj_kernel, alpha=alpha),
        grid=(B, nt),
        in_specs=[pl.BlockSpec((1, tm, o_nsa.shape[2]), lambda b, i: (b, i, 0)),
                  pl.BlockSpec((1, tm, o_gla.shape[2]), lambda b, i: (b, i, 0)),
                  pl.BlockSpec((1, tm, D), lambda b, i: (b, i, 0)),
                  pl.BlockSpec((1, 1, D), lambda b, i: (b * 6 + 2, 0, 0)),
                  pl.BlockSpec((1, 1, D), lambda b, i: (b * 6 + 3, 0, 0)),
                  pl.BlockSpec((1, 1, D), lambda b, i: (b * 6 + 4, 0, 0)),
                  pl.BlockSpec((1, D), const2),
                  pl.BlockSpec((1, D), const2),
                  pl.BlockSpec(wo_pad.shape, const2),
                  pl.BlockSpec(wr_hi.shape, const2),
                  pl.BlockSpec(wr_lo.shape, const2),
                  pl.BlockSpec(rbias.shape, const2),
                  pl.BlockSpec(tri.shape, const2)],
        out_specs=[pl.BlockSpec((1, tm, D), lambda b, i: (b, i, 0)),
                   pl.BlockSpec((tm, D), lambda b, i: (b * nt + i, 0)),
                   pl.BlockSpec((TOP_K, tm), tok),
                   pl.BlockSpec((TOP_K, tm), tok),
                   pl.BlockSpec((TOP_K, tm), tok),
                   pl.BlockSpec((E, LANES), const2)],
        out_shape=[jax.ShapeDtypeStruct((B, T, D), F32),
                   jax.ShapeDtypeStruct((N, D), F32),
                   jax.ShapeDtypeStruct((TOP_K, N), jnp.int32),
                   jax.ShapeDtypeStruct((TOP_K, N), F32),
                   jax.ShapeDtypeStruct((TOP_K, N), jnp.int32),
                   jax.ShapeDtypeStruct((E, LANES), F32)],
        scratch_shapes=[pltpu.VMEM((E, LANES), F32)],
        compiler_params=_cparams(2),
        name="oproj",
    )(o_nsa, o_gla, x, mod3, mod3, mod3, ln_g, ln_b, wo_pad, wr_hi, wr_lo, rbias, tri)


def _slots_kernel(pstart_ref, eidx_ref, pos_ref, dest_ref):
    eidx = eidx_ref[...]

    def body(e, acc):
        return acc + jnp.where(eidx == e, pstart_ref[e], 0)

    dest_ref[...] = lax.fori_loop(0, pstart_ref.shape[0], body, pos_ref[...])


def _slots_call(pstart, eidx, pos):
    K, N = eidx.shape
    tm = min(TM_SLOTS, N)
    grid_spec = pltpu.PrefetchScalarGridSpec(
        num_scalar_prefetch=1,
        grid=(N // tm,),
        in_specs=[pl.BlockSpec((K, tm), lambda i, *_: (0, i)),
                  pl.BlockSpec((K, tm), lambda i, *_: (0, i))],
        out_specs=pl.BlockSpec((K, tm), lambda i, *_: (0, i)),
    )
    return pl.pallas_call(
        _slots_kernel,
        grid_spec=grid_spec,
        out_shape=jax.ShapeDtypeStruct((K, N), jnp.int32),
        compiler_params=_cparams(1),
        name="slots",
    )(pstart, eidx, pos)


def _row_copy(src_ref, src_row, dst_ref, dst_row, sem):
    return pltpu.make_async_copy(src_ref.at[pl.ds(src_row, 1)], dst_ref.at[pl.ds(dst_row, 1)], sem)


def _dispatch_kernel(pstart_ref, pend_ref, dest_ref, h2p_ref, xs_ref, zero_ref, sem, zsem):
    step = pl.program_id(0)
    tm = h2p_ref.shape[0]
    n_exp = pstart_ref.shape[0]

    @pl.when(step == 0)
    def _():
        zero_ref[...] = jnp.zeros_like(zero_ref)
        n_tail = xs_ref.shape[0] // MOE_BLOCK - pend_ref[n_exp - 1] // MOE_BLOCK

        def zero_block(row0):
            return pltpu.make_async_copy(
                zero_ref, xs_ref.at[pl.ds(pl.multiple_of(row0, MOE_BLOCK), MOE_BLOCK)], zsem)

        def start(e, c):
            @pl.when(pend_ref[e] > pstart_ref[e])
            def _():
                zero_block(pend_ref[e] - MOE_BLOCK).start()
            return c

        def wait(e, c):
            @pl.when(pend_ref[e] > pstart_ref[e])
            def _():
                zero_block(0).wait()
            return c

        def tail_start(j, c):
            zero_block(pend_ref[n_exp - 1] + j * MOE_BLOCK).start()
            return c

        def tail_wait(j, c):
            zero_block(0).wait()
            return c

        lax.fori_loop(0, n_exp, start, 0)
        lax.fori_loop(0, n_tail, tail_start, 0)
        lax.fori_loop(0, n_exp, wait, 0)
        lax.fori_loop(0, n_tail, tail_wait, 0)

    def start(t, c):
        for k in range(TOP_K):
            _row_copy(h2p_ref, t, xs_ref, dest_ref[k, t], sem).start(priority=k % N_DMA_QUEUES)
        return c

    def wait(t, c):
        for k in range(TOP_K):
            _row_copy(h2p_ref, t, xs_ref, 0, sem).wait()
        return c

    lax.fori_loop(0, tm, start, 0)
    lax.fori_loop(0, tm, wait, 0)


def _dispatch_call(pstart, pend, dest, h2p, n_rows):
    N = h2p.shape[0]
    tile = h2p.shape[1:]
    tm = min(TM_DISP, N)
    grid_spec = pltpu.PrefetchScalarGridSpec(
        num_scalar_prefetch=2,
        grid=(N // tm,),
        in_specs=[pl.BlockSpec((TOP_K, tm), lambda i, *_: (0, i), memory_space=pltpu.SMEM),
                  pl.BlockSpec((tm,) + tile, lambda i, *_: (i, 0))],
        out_specs=pl.BlockSpec(memory_space=pl.ANY),
        scratch_shapes=[pltpu.VMEM((MOE_BLOCK,) + tile, h2p.dtype),
                        pltpu.SemaphoreType.DMA(()),
                        pltpu.SemaphoreType.DMA(())],
    )
    return pl.pallas_call(
        _dispatch_kernel,
        grid_spec=grid_spec,
        out_shape=jax.ShapeDtypeStruct((n_rows,) + tile, h2p.dtype),
        compiler_params=_cparams(1),
        name="dispatch",
    )(pstart, pend, dest, h2p)


def _experts_kernel(pstart_ref, pend_ref, xs_ref, wg_ref, wu_ref, wd_ref, ys_ref,
                    xbuf, ybuf, wgu_s, wd_s, xsem, ysem):
    e = pl.program_id(0)
    n_exp = pl.num_programs(0)
    hd = wg_ref.shape[2]
    n_slots, M = xbuf.shape[0], xbuf.shape[1]
    assert n_slots & (n_slots - 1) == 0
    ahead = n_slots - 1
    n_used = pend_ref[n_exp - 1] // M
    n_total = xs_ref.shape[0] // M

    def rows(b):
        return pl.ds(pl.multiple_of(b * M, M), M)

    def x_copy(b):
        slot = b & (n_slots - 1)
        return pltpu.make_async_copy(xs_ref.at[rows(b)], xbuf.at[slot], xsem.at[slot])

    def y_copy(b):
        slot = b & (n_slots - 1)
        return pltpu.make_async_copy(ybuf.at[slot], ys_ref.at[rows(b)], ysem.at[slot])

    @pl.when(e == 0)
    def _():
        for j in range(ahead):
            @pl.when(j < n_used)
            def _():
                x_copy(j).start()

    b0 = pstart_ref[e] // M
    nb = pend_ref[e] // M - b0

    @pl.when(nb > 0)
    def _():
        wgu_s[:, 0:hd] = wg_ref[0].astype(BF16)
        wgu_s[:, hd:2 * hd] = wu_ref[0].astype(BF16)
        wd_s[...] = wd_ref[0].astype(BF16)

    def block(j, c):
        b = b0 + j
        slot = b & (n_slots - 1)
        x_copy(b).wait()

        @pl.when(b + ahead < n_used)
        def _():
            x_copy(b + ahead).start()

        xb = xbuf[slot].astype(BF16)
        h = _dot(xb, wgu_s[...])
        hid = (_silu(h[:, 0:hd]) * h[:, hd:2 * hd]).astype(BF16)
        y = _dot(hid, wd_s[...])

        @pl.when(b >= n_slots)
        def _():
            y_copy(b - n_slots).wait()

        ybuf[slot] = y
        y_copy(b).start()
        return c

    lax.fori_loop(0, nb, block, 0)

    @pl.when(e == n_exp - 1)
    def _():
        for j in range(1, n_slots + 1):
            @pl.when(n_used >= j)
            def _():
                y_copy(n_used - j).wait()

        ybuf[0] = jnp.zeros(ybuf.shape[1:], ybuf.dtype)

        def tail_copy(j):
            return pltpu.make_async_copy(ybuf.at[0], ys_ref.at[rows(n_used + j)], ysem.at[0])

        def tail_start(j, c):
            tail_copy(j).start()
            return c

        def tail_wait(j, c):
            tail_copy(j).wait()
            return c

        lax.fori_loop(0, n_total - n_used, tail_start, 0)
        lax.fori_loop(0, n_total - n_used, tail_wait, 0)


def _experts_call(pstart, pend, xs, w_gate, w_up, w_down):
    n_rows = xs.shape[0]
    tile = xs.shape[1:]
    E, D, Hd = w_gate.shape
    wmap = lambda e, *_: (e, 0, 0)
    grid_spec = pltpu.PrefetchScalarGridSpec(
        num_scalar_prefetch=2,
        grid=(E,),
        in_specs=[pl.BlockSpec(memory_space=pl.ANY),
                  pl.BlockSpec((1, D, Hd), wmap),
                  pl.BlockSpec((1, D, Hd), wmap),
                  pl.BlockSpec((1, Hd, D), wmap)],
        out_specs=pl.BlockSpec(memory_space=pl.ANY),
        scratch_shapes=[pltpu.VMEM((EXPERT_RING, MOE_BLOCK) + tile, xs.dtype),
                        pltpu.VMEM((EXPERT_RING, MOE_BLOCK) + tile, F32),
                        pltpu.VMEM((D, 2 * Hd), BF16),
                        pltpu.VMEM((Hd, D), BF16),
                        pltpu.SemaphoreType.DMA((EXPERT_RING,)),
                        pltpu.SemaphoreType.DMA((EXPERT_RING,))],
    )
    return pl.pallas_call(
        _experts_kernel,
        grid_spec=grid_spec,
        out_shape=jax.ShapeDtypeStruct((n_rows,) + tile, F32),
        compiler_params=_cparams(1),
        name="experts",
    )(pstart, pend, xs, w_gate, w_up, w_down)


def _combine_kernel(dest_ref, ys_ref, wgt_ref, h2_ref, x1_ref, ga_ref,
                    g2_ref, b2_ref, wsg_ref, wsu_ref, wsd_ref, o_ref, buf_ref, sem, *, alpha):
    tm = h2_ref.shape[0]

    def start(t, c):
        for k in range(TOP_K):
            _row_copy(ys_ref, dest_ref[k, t], buf_ref.at[k], t, sem).start(priority=k % N_DMA_QUEUES)
        return c

    def wait(t, c):
        for k in range(TOP_K):
            _row_copy(ys_ref, 0, buf_ref.at[k], t, sem).wait()
        return c

    lax.fori_loop(0, tm, start, 0)
    hb = h2_ref[...].astype(BF16)
    hid = (_silu(_dot(hb, wsg_ref[...])) * _dot(hb, wsu_ref[...])).astype(BF16)
    y = _dot(hid, wsd_ref[...])
    lax.fori_loop(0, tm, wait, 0)
    wgt = wgt_ref[...]
    for k in range(TOP_K):
        y = y + buf_ref[k] * wgt[:, k:k + 1]
    o_ref[0] = _normalize_rows(alpha * x1_ref[0] + (1.0 + ga_ref[0]) * y) * g2_ref[...] + b2_ref[...]


def _combine_call(dest, ys, wgt_t, h2, x1, mod3, ln_g, ln_b, wsg, wsu, wsd, alpha):
    B, T, D = x1.shape
    tile = ys.shape[1:]
    tm = min(TM_COMB, T)
    nt = T // tm
    tokc = lambda b, i: (0, b * nt + i)
    tokr = lambda b, i: (b * nt + i, 0)
    row = lambda b, i: (b, i, 0)
    const2 = lambda b, i: (0, 0)
    grid_spec = pltpu.PrefetchScalarGridSpec(
        num_scalar_prefetch=0,
        grid=(B, nt),
        in_specs=[pl.BlockSpec((TOP_K, tm), tokc, memory_space=pltpu.SMEM),
                  pl.BlockSpec(memory_space=pl.ANY),
                  pl.BlockSpec((tm, TOP_K), tokr),
                  pl.BlockSpec((tm,) + tile, lambda b, i: (b * nt + i, 0)),
                  pl.BlockSpec((1, tm, D), row),
                  pl.BlockSpec((1, 1, D), lambda b, i: (b * 6 + 5, 0, 0)),
                  pl.BlockSpec((1, D), const2),
                  pl.BlockSpec((1, D), const2),
                  pl.BlockSpec(wsg.shape, const2),
                  pl.BlockSpec(wsu.shape, const2),
                  pl.BlockSpec(wsd.shape, const2)],
        out_specs=pl.BlockSpec((1, tm, D), row),
        scratch_shapes=[pltpu.VMEM((TOP_K, tm) + tile, F32),
                        pltpu.SemaphoreType.DMA(())],
    )
    return pl.pallas_call(
        functools.partial(_combine_kernel, alpha=alpha),
        grid_spec=grid_spec,
        out_shape=jax.ShapeDtypeStruct((B, T, D), F32),
        compiler_params=_cparams(2),
        name="combine",
    )(dest, ys, wgt_t, h2, x1, mod3, ln_g, ln_b, wsg, wsu, wsd)


def _pad_heads(w, n_heads, dh, side):
    D = w.shape[0]
    w = w.reshape(D, n_heads, dh)
    z = jnp.zeros_like(w)
    if side is None:
        out = jnp.concatenate([w, z], axis=-1)
    else:
        lo = jnp.concatenate([w, z], axis=-1)
        hi = jnp.concatenate([z, w], axis=-1)
        out = jnp.where(side[None, :, None], hi, lo)
    return out.reshape(D, n_heads * 2 * dh)


def _layout_w_in(w_in):
    D = w_in.shape[0]
    H, dh, G = NSA_HEADS, NSA_HEAD_DIM, NSA_KV_GROUPS
    o = 0
    wq = w_in[:, o:o + H * dh]; o += H * dh
    wcmp = w_in[:, o:o + 2 * G * dh]; o += 2 * G * dh
    wkvn = w_in[:, o:o + 4 * G * dh]; o += 4 * G * dh
    wng = w_in[:, o:o + H * 3]; o += H * 3
    wgq = w_in[:, o:o + GLA_HEADS * GLA_DK]; o += GLA_HEADS * GLA_DK
    wgk = w_in[:, o:o + GLA_HEADS * GLA_DK]; o += GLA_HEADS * GLA_DK
    wgv = w_in[:, o:o + GLA_HEADS * GLA_DV]; o += GLA_HEADS * GLA_DV
    wglr = w_in[:, o:o + GLA_GATE_RANK]; o += GLA_GATE_RANK
    wgr = w_in[:, o:o + GLA_HEADS * GLA_DV]; o += GLA_HEADS * GLA_DV
    high_half = (jnp.arange(H) // NSA_HPG) == 1
    wqn = _pad_heads(wq * (dh ** -0.5), H, dh, high_half)
    ng = wng.reshape(D, G, NSA_HPG * 3)
    wgate = jnp.concatenate([ng, jnp.zeros((D, G, LANES - NSA_HPG * 3), w_in.dtype)], -1).reshape(D, G * LANES)
    wgqk = jnp.concatenate([_pad_heads(wgq * (GLA_DK ** -0.5), GLA_HEADS, GLA_DK, None),
                            _pad_heads(wgk, GLA_HEADS, GLA_DK, None)], axis=1)
    wglr_p = jnp.concatenate([wglr, jnp.zeros((D, LANES - GLA_GATE_RANK), w_in.dtype)], axis=1)
    w_all = jnp.concatenate([wqn, wkvn, wcmp, wgate, wgqk, wgv, wglr_p, wgr], axis=1).astype(BF16)
    return w_all, wgv.T.astype(BF16)


def _overlap_matrix(n_cp, n_sp, seq_len):
    n_cmp = seq_len // CMP_STRIDE - CMP_BLOCK // CMP_STRIDE + 1
    n_slc = seq_len // SLC_BLOCK
    cs = np.arange(n_cmp) * CMP_STRIDE
    ce = cs + CMP_BLOCK - 1
    ss = np.arange(n_slc) * SLC_BLOCK
    se = ss + SLC_BLOCK - 1
    ov = np.clip(np.minimum(ce[:, None], se[None]) - np.maximum(cs[:, None], ss[None]) + 1, 0, None)
    out = np.zeros((n_cp, n_sp), np.float32)
    out[:n_cmp, :n_slc] = ov.astype(np.float32) / CMP_BLOCK
    return jnp.asarray(out, dtype=BF16)


def _round_up(a, m):
    return (a + m - 1) // m * m


def _token_mixer(x, mod3, w_in, cmp_pos_k, cmp_w1_k, cmp_w2_k, cmp_pos_v, cmp_w1_v, cmp_w2_v,
                 gla_w_a2, gla_b_a2, gla_norm_w):
    B, T, D = x.shape
    G, dh = NSA_KV_GROUPS, NSA_HEAD_DIM
    w_all, w_vt = _layout_w_in(w_in)
    qn, kvn, cmpkv, gates, gqk, gv, glr, gr, gvt = _inproj_call(x, mod3, w_all, w_vt)

    n_sub = T // CMP_STRIDE
    pos2 = jnp.tile(jnp.stack([cmp_pos_k, cmp_pos_v]), (1, 1, G))
    w1 = jnp.stack([cmp_w1_k, cmp_w1_v])
    w2 = jnp.stack([cmp_w2_k, cmp_w2_v])
    w2p = jnp.stack([jnp.pad(w2, ((0, 0), (0, 0), (g * dh, (G - 1 - g) * dh))) for g in range(G)], axis=1)
    kcvc = _compress_call(cmpkv, pos2, w1, w2p)
    n_cp = _round_up(n_sub, LANES)
    kcvc = jnp.pad(kcvc, ((0, 0), (0, 0), (0, n_cp - n_sub), (0, 0)))
    n_slc = T // SLC_BLOCK
    n_sp = _round_up(n_slc, LANES)
    cmp_last = np.arange(n_cp) * CMP_STRIDE + (CMP_BLOCK - 1)
    cfeat = _position_features(cmp_last, n_slc)
    cfeat[:, :n_slc] = 0.0
    kcx = jnp.concatenate([kcvc[0], jnp.broadcast_to(jnp.asarray(cfeat, BF16), (B, n_cp, LANES))], axis=2)
    kfeat = jnp.asarray(_position_features(np.arange(T), n_slc), BF16)
    o_nsa = _nsa_call(qn, gates, kcx, kcvc[1], kvn, kfeat, _overlap_matrix(n_cp, n_sp, T))

    wa = gla_w_a2.reshape(GLA_GATE_RANK, GLA_HEADS, GLA_DK)
    wa_pad = jnp.zeros((LANES, GLA_HEADS, LANES), F32).at[:GLA_GATE_RANK, :, :GLA_DK].set(wa)
    wa_pad = wa_pad.reshape(LANES, GLA_HEADS * LANES)
    ba_pad = jnp.zeros((GLA_HEADS, LANES), F32).at[:, :GLA_DK].set(gla_b_a2.reshape(GLA_HEADS, GLA_DK))
    ba_pad = ba_pad.reshape(1, GLA_HEADS * LANES)
    o_gla = _gla_call(gqk, gv, gvt, glr, gr, wa_pad, ba_pad, gla_norm_w.reshape(1, GLA_DV))
    return o_nsa, o_gla


def _layer(x, c_pad, l, alpha, w_mod, b_mod, w_in, cmp_pos_k, cmp_w1_k, cmp_w2_k, cmp_pos_v, cmp_w1_v,
           cmp_w2_v, gla_w_a2, gla_b_a2, gla_norm_w, w_o, ln1_g, ln1_b, w_router, router_bias,
           w_e_gate, w_e_up, w_e_down, w_s_gate, w_s_up, w_s_down, ln2_g, ln2_b):
    B, T, D = x.shape
    N = B * T
    E = N_EXPERTS
    mod = _mod_call(c_pad, w_mod[l], b_mod[l].reshape(1, -1))
    mod3 = mod[:B].reshape(B * 6, 1, D)

    o_nsa, o_gla = _token_mixer(x, mod3, w_in[l], cmp_pos_k[l], cmp_w1_k[l], cmp_w2_k[l], cmp_pos_v[l],
                                cmp_w1_v[l], cmp_w2_v[l], gla_w_a2[l], gla_b_a2[l], gla_norm_w[l])

    wo = w_o[l]
    n_nsa = NSA_HEADS * NSA_HEAD_DIM
    high_half = (jnp.arange(NSA_HEADS) // NSA_HPG) == 1
    wo_nsa = _pad_heads(wo[:n_nsa].T, NSA_HEADS, NSA_HEAD_DIM, high_half).T
    wo_pad = jnp.concatenate([wo_nsa, wo[n_nsa:]], axis=0).astype(BF16)
    wr_t = w_router[l].T
    wr_hi = wr_t.astype(BF16)
    wr_lo = (wr_t - wr_hi.astype(F32)).astype(BF16)
    tm_o = min(TM_OPROJ, T)
    tri = jnp.asarray(np.triu(np.ones((tm_o, tm_o), np.float32), 1), dtype=BF16)
    x1, h2, eidx, wgt, pos, counts = _oproj_call(
        o_nsa, o_gla, x, mod3, ln1_g[l].reshape(1, D), ln1_b[l].reshape(1, D), wo_pad, wr_hi, wr_lo,
        router_bias[l].reshape(E, 1), tri, alpha)

    cnt = counts[:, 0].astype(jnp.int32)
    padded = (cnt + MOE_BLOCK - 1) // MOE_BLOCK * MOE_BLOCK
    pend = jnp.cumsum(padded)
    pstart = pend - padded
    n_blocks = -(-N * TOP_K // MOE_BLOCK) + E

    dest = _slots_call(pstart, eidx, pos)
    xs = _dispatch_call(pstart, pend, dest, h2, n_blocks * MOE_BLOCK)
    ys = _experts_call(pstart, pend, xs, w_e_gate[l], w_e_up[l], w_e_down[l])
    return _combine_call(dest, ys, wgt.T, h2, x1, mod3, ln2_g[l].reshape(1, D),
                         ln2_b[l].reshape(1, D), w_s_gate[l].astype(BF16), w_s_up[l].astype(BF16),
                         w_s_down[l].astype(BF16), alpha)


def kernel(x, c, w_mod, b_mod, w_in, cmp_pos_k, cmp_w1_k, cmp_w2_k, cmp_pos_v, cmp_w1_v, cmp_w2_v, gla_w_a2, gla_b_a2, gla_norm_w, w_o, ln1_g, ln1_b, w_router, router_bias, w_e_gate, w_e_up, w_e_down, w_s_gate, w_s_up, w_s_down, ln2_g, ln2_b):
    depth = w_mod.shape[0]
    alpha = (2.0 * depth) ** 0.25
    B = x.shape[0]
    c_pad = jnp.pad(c, ((0, (-B) % 8), (0, 0)))
    for l in range(depth):
        x = _layer(x, c_pad, l, alpha, w_mod, b_mod, w_in, cmp_pos_k, cmp_w1_k, cmp_w2_k, cmp_pos_v,
                   cmp_w1_v, cmp_w2_v, gla_w_a2, gla_b_a2, gla_norm_w, w_o, ln1_g, ln1_b, w_router,
                   router_bias, w_e_gate, w_e_up, w_e_down, w_s_gate, w_s_up, w_s_down, ln2_g, ln2_b)
    return x
```
